```python
import jax, jax.numpy as jnp
from jax import lax
import numpy as np

D_MODEL = 1024
BATCH = 4
SEQ = 4096
DEPTH = 2
DEC_BATCH = 32
DEC_SEQ = 1
PAST_LEN = 16384
PAGE_SIZE = 128

MIX_WIDTH = D_MODEL
W_A = MIX_WIDTH // 4
W_B = MIX_WIDTH // 4
W_C = MIX_WIDTH // 4
W_D = MIX_WIDTH - W_A - W_B - W_C
HEAD_DIM = 64
A_HEADS = W_A // HEAD_DIM
CHUNK = 128
ATT_HEADS = W_B // HEAD_DIM
IDX_HEADS = 8
IDX_DIM = 64
TOPK_MAX = 256
QBLK = 128
ROPE_THETA = 10000.0
POOL_WINDOWS = (2, 4, 8, 16)
POOL_GROUPS = len(POOL_WINDOWS)
POOL_GW = W_C // POOL_GROUPS
POOL_STATE = max(POOL_WINDOWS) - 1
CONV_WIDTH = 31
CONV_STATE = CONV_WIDTH - 1
CONV_GROUPS = 4
D_FF = 4 * D_MODEL
EPS = 1e-6
IN_SIZES = (W_A, W_A, W_B, W_B, W_B, IDX_HEADS * IDX_DIM, IDX_DIM, IDX_HEADS, W_C, W_D, W_D)
D_IN = sum(IN_SIZES)

kernel_name = 'hymba_style_gmlp_dsa_pool_conformer_decode'


def rmsnorm(x, g):
    xf = x.astype(jnp.float32)
    y = xf * lax.rsqrt(jnp.mean(xf * xf, axis=-1, keepdims=True) + EPS)
    return (y * g.astype(jnp.float32)).astype(x.dtype)


def rope(x, pos):
    half = x.shape[-1] // 2
    freqs = ROPE_THETA ** (-jnp.arange(half, dtype=jnp.float32) / half)
    ang = pos.astype(jnp.float32)[:, None] * freqs[None, :]
    cos = jnp.cos(ang)[:, None, :]
    sin = jnp.sin(ang)[:, None, :]
    xf = x.astype(jnp.float32)
    x1, x2 = xf[..., :half], xf[..., half:]
    return jnp.concatenate([x1 * cos - x2 * sin, x2 * cos + x1 * sin], axis=-1).astype(x.dtype)


def _split_in(z):
    outs, start = [], 0
    for n in IN_SIZES:
        outs.append(z[..., start:start + n])
        start += n
    return outs


def _modulate(x, g, shift, scale):
    return rmsnorm(x, g) * (1 + scale[:, None, :]) + shift[:, None, :]


def _gmlp_chunk(u, v_n, w_spatial, b_spatial, sample):
    B, T, _ = u.shape
    mask = jnp.tril(jnp.ones((CHUNK, CHUNK), dtype=bool))
    ws = jnp.where(mask[None], w_spatial, 0).astype(v_n.dtype)
    if sample:
        vh = v_n.reshape(B, T, A_HEADS, HEAD_DIM)
        s = jnp.einsum('hts,bshd->bthd', ws[:, :T, :T], vh) + b_spatial[:, :T].T[None, :, :, None]
    else:
        vh = v_n.reshape(B, T // CHUNK, CHUNK, A_HEADS, HEAD_DIM)
        s = jnp.einsum('hts,bcshd->bcthd', ws, vh) + b_spatial.T[None, None, :, :, None]
    return u * s.reshape(B, T, W_A)


def _indexer_scores(qi, wi, ki, mask):
    dots = jnp.einsum('bthd,bsd->bths', qi, ki).astype(jnp.float32)
    s = jnp.einsum('bths,bth->bts', jax.nn.relu(dots), wi.astype(jnp.float32)) * (IDX_HEADS * IDX_DIM) ** -0.5
    return jnp.where(mask, s, -jnp.inf)


def _sparse_attend(q, kg, vg, valid):
    logits = jnp.einsum('bthd,btkhd->bhtk', q, kg).astype(jnp.float32) * HEAD_DIM ** -0.5
    logits = jnp.where(valid[:, None], logits, -jnp.inf)
    p = jax.nn.softmax(logits, axis=-1)
    return jnp.einsum('bhtk,btkhd->bthd', p.astype(vg.dtype), vg)


def _dsa_prompt(q, k, v, qi, wi, ki):
    B, S = q.shape[0], q.shape[1]
    topk = min(TOPK_MAX, S // 4)
    n_blk = S // QBLK
    key_pos = jnp.arange(S)
    bidx = jnp.arange(B)[:, None, None]

    def block(i):
        start = i * QBLK
        qb = lax.dynamic_slice_in_dim(q, start, QBLK, axis=1)
        qib = lax.dynamic_slice_in_dim(qi, start, QBLK, axis=1)
        wib = lax.dynamic_slice_in_dim(wi, start, QBLK, axis=1)
        q_pos = start + jnp.arange(QBLK)
        adm = key_pos[None, :] <= q_pos[:, None]
        sc = _indexer_scores(qib, wib, ki, adm[None])
        _, idx = lax.top_k(sc, topk)
        valid = idx <= q_pos[None, :, None]
        return _sparse_attend(qb, k[bidx, idx], v[bidx, idx], valid)

    o = lax.map(block, jnp.arange(n_blk))
    return jnp.moveaxis(o, 0, 1).reshape(B, S, W_B)


def _dsa_sample(q, k, v, qi, wi, ki, ck, cv, cki, page_table):
    B, T = q.shape[0], q.shape[1]
    n_pages = page_table.shape[1]
    past = n_pages * PAGE_SIZE
    L = past + T
    topk = min(TOPK_MAX, L // 4)
    ki_past = cki[page_table].reshape(B, past, IDX_DIM)
    ki_all = jnp.concatenate([ki_past, ki], axis=1)
    q_pos = past + jnp.arange(T)
    adm = jnp.arange(L)[None, :] <= q_pos[:, None]
    sc = _indexer_scores(qi, wi, ki_all, adm[None])
    _, idx = lax.top_k(sc, topk)
    bidx = jnp.arange(B)[:, None, None]
    lp = jnp.minimum(idx, past - 1)
    phys = page_table[bidx, lp // PAGE_SIZE]
    row = lp % PAGE_SIZE
    jn = jnp.clip(idx - past, 0, T - 1)
    is_new = (idx >= past)[..., None, None]
    kg = jnp.where(is_new, k[bidx, jn], ck[phys, row])
    vg = jnp.where(is_new, v[bidx, jn], cv[phys, row])
    valid = idx <= q_pos[None, :, None]
    return _sparse_attend(q, kg, vg, valid).reshape(B, T, W_B)


def _multiscale_pool(xc, prev, w_pool, s_pool):
    B, T, _ = xc.shape
    P = prev.shape[1]
    full = jnp.concatenate([prev, xc], axis=1)
    cs = jnp.concatenate([jnp.zeros((B, 1, W_C), jnp.float32), jnp.cumsum(full.astype(jnp.float32), axis=1)], axis=1)
    hi = P + 1 + jnp.arange(T)
    pooled = []
    for g, w in enumerate(POOL_WINDOWS):
        lo = jnp.maximum(hi - w, 0)
        csg = cs[:, :, g * POOL_GW:(g + 1) * POOL_GW]
        s = jnp.take(csg, hi, axis=1) - jnp.take(csg, lo, axis=1)
        pooled.append(s / (hi - lo).astype(jnp.float32)[None, :, None])
    pooled = jnp.concatenate(pooled, axis=-1)
    diff = (pooled - xc.astype(jnp.float32)).astype(xc.dtype).reshape(B, T, POOL_GROUPS, POOL_GW)
    y = jnp.einsum('btgc,gcd->btgd', diff, w_pool).reshape(B, T, W_C) * s_pool
    return y, full[:, -POOL_STATE:]


def _conformer_conv(a, gate, prev, w_dw, b_dw, g_n, b_n):
    glu = a * jax.nn.sigmoid(gate)
    full = jnp.concatenate([prev, glu], axis=1)
    y = lax.conv_general_dilated(full, w_dw[:, None, :], window_strides=(1,), padding='VALID',
                                 dimension_numbers=('NWC', 'WIO', 'NWC'), feature_group_count=W_D) + b_dw
    B, T, _ = y.shape
    yg = y.reshape(B, T, CONV_GROUPS, W_D // CONV_GROUPS).astype(jnp.float32)
    mu = jnp.mean(yg, axis=-1, keepdims=True)
    var = jnp.mean(jnp.square(yg - mu), axis=-1, keepdims=True)
    yn = (yg - mu) * lax.rsqrt(var + EPS) * g_n.astype(jnp.float32) + b_n.astype(jnp.float32)
    return jax.nn.silu(yn).astype(a.dtype).reshape(B, T, W_D), full[:, -CONV_STATE:]


def _layer(x, c, pos, sample, lp, kv_ctx, pool_prev, conv_prev):
    (w_mod, b_mod, g_attn_norm, g_mlp_norm, w_in, w_out, g_gmlp, w_spatial, b_spatial,
     w_pool, s_pool, w_dw, b_dw, g_conv_norm, b_conv_norm, w_up, w_down) = lp
    B, T, _ = x.shape
    mod = jax.nn.silu(c) @ w_mod + b_mod
    sh1, sc1, g1, sh2, sc2, g2 = jnp.split(mod, 6, axis=-1)
    h = _modulate(x, g_attn_norm, sh1, sc1)
    a_u, a_v, b_q, b_k, b_v, b_qi, b_ki, b_w, c_x, d_a, d_g = _split_in(h @ w_in)
    v_n = rmsnorm(a_v.reshape(B, T, A_HEADS, HEAD_DIM), g_gmlp).reshape(B, T, W_A)
    out_a = _gmlp_chunk(a_u, v_n, w_spatial, b_spatial, sample)
    q = rope(b_q.reshape(B, T, ATT_HEADS, HEAD_DIM), pos)
    k = rope(b_k.reshape(B, T, ATT_HEADS, HEAD_DIM), pos)
    v = b_v.reshape(B, T, ATT_HEADS, HEAD_DIM)
    qi = rope(b_qi.reshape(B, T, IDX_HEADS, IDX_DIM), pos)
    ki = rope(b_ki[:, :, None, :], pos)[:, :, 0, :]
    if sample:
        ck, cv, cki, page_table = kv_ctx
        out_b = _dsa_sample(q, k, v, qi, b_w, ki, ck, cv, cki, page_table)
    else:
        out_b = _dsa_prompt(q, k, v, qi, b_w, ki)
    out_c, pool_new = _multiscale_pool(c_x, pool_prev, w_pool, s_pool)
    out_d, conv_new = _conformer_conv(d_a, d_g, conv_prev, w_dw, b_dw, g_conv_norm, b_conv_norm)
    mixed = jnp.concatenate([out_a, out_b, out_c, out_d], axis=-1) @ w_out
    x = x + g1[:, None, :] * mixed
    h = _modulate(x, g_mlp_norm, sh2, sc2)
    x = x + g2[:, None, :] * (jnp.square(jax.nn.relu(h @ w_up)) @ w_down)
    return x, (k, v, ki, v_n, pool_new, conv_new)


def setup_inputs(seed: int = 0) -> dict:
    key = jax.random.key(seed)
    ks = jax.random.split(key, 32)
    f32 = jnp.float32
    n_pages = PAST_LEN // PAGE_SIZE
    n_used = DEC_BATCH * n_pages
    n_pool = n_used + n_used // 4

    def nrm(k, shape, scale):
        return jax.random.normal(k, shape, f32) * scale

    page_table = jax.random.permutation(ks[0], n_pool)[:n_used].reshape(DEC_BATCH, n_pages).astype(jnp.int32)
    return {
        'x_prompt': nrm(ks[1], (BATCH, SEQ, D_MODEL), 1.0),
        'x_sample': nrm(ks[2], (DEC_BATCH, DEC_SEQ, D_MODEL), 1.0),
        'cache_k': nrm(ks[3], (DEPTH, n_pool, PAGE_SIZE, ATT_HEADS, HEAD_DIM), 1.0),
        'cache_v': nrm(ks[4], (DEPTH, n_pool, PAGE_SIZE, ATT_HEADS, HEAD_DIM), 1.0),
        'cache_kidx': nrm(ks[5], (DEPTH, n_pool, PAGE_SIZE, IDX_DIM), 1.0),
        'state_pool': nrm(ks[6], (DEPTH, DEC_BATCH, POOL_STATE, W_C), 1.0),
        'state_conv': nrm(ks[7], (DEPTH, DEC_BATCH, CONV_STATE, W_D), 1.0),
        'page_table': page_table,
        'c_prompt': nrm(ks[8], (BATCH, D_MODEL), 1.0),
        'c_sample': nrm(ks[9], (DEC_BATCH, D_MODEL), 1.0),
        'w_mod': nrm(ks[10], (DEPTH, D_MODEL, 6 * D_MODEL), 0.2 * D_MODEL ** -0.5),
        'b_mod': nrm(ks[11], (DEPTH, 6 * D_MODEL), 0.02),
        'g_attn_norm': 1.0 + nrm(ks[12], (DEPTH, D_MODEL), 0.02),
        'g_mlp_norm': 1.0 + nrm(ks[13], (DEPTH, D_MODEL), 0.02),
        'w_in': nrm(ks[14], (DEPTH, D_MODEL, D_IN), D_MODEL ** -0.5),
        'w_out': nrm(ks[15], (DEPTH, MIX_WIDTH, D_MODEL), MIX_WIDTH ** -0.5),
        'g_gmlp': 1.0 + nrm(ks[16], (DEPTH, A_HEADS, HEAD_DIM), 0.02),
        'w_spatial': nrm(ks[17], (DEPTH, A_HEADS, CHUNK, CHUNK), CHUNK ** -0.5),
        'b_spatial': 1.0 + nrm(ks[18], (DEPTH, A_HEADS, CHUNK), 0.02),
        'w_pool': nrm(ks[19], (DEPTH, POOL_GROUPS, POOL_GW, POOL_GW), POOL_GW ** -0.5),
        's_pool': 1.0 + nrm(ks[20], (DEPTH, W_C), 0.02),
        'w_dw': nrm(ks[21], (DEPTH, CONV_WIDTH, W_D), CONV_WIDTH ** -0.5),
        'b_dw': nrm(ks[22], (DEPTH, W_D), 0.02),
        'g_conv_norm': 1.0 + nrm(ks[23], (DEPTH, CONV_GROUPS, W_D // CONV_GROUPS), 0.02),
        'b_conv_norm': nrm(ks[24], (DEPTH, CONV_GROUPS, W_D // CONV_GROUPS), 0.02),
        'w_up': nrm(ks[25], (DEPTH, D_MODEL, D_FF), D_MODEL ** -0.5),
        'w_down': nrm(ks[26], (DEPTH, D_FF, D_MODEL), D_FF ** -0.5),
        'g_final': 1.0 + nrm(ks[27], (D_MODEL,), 0.02),
    }


def reference(x_prompt, x_sample, cache_k, cache_v, cache_kidx, state_pool, state_conv, page_table,
              c_prompt, c_sample, w_mod, b_mod, g_attn_norm, g_mlp_norm, w_in, w_out, g_gmlp,
              w_spatial, b_spatial, w_pool, s_pool, w_dw, b_dw, g_conv_norm, b_conv_norm,
              w_up, w_down, g_final):
    xp, xs = x_prompt, x_sample
    bp = x_prompt.shape[0]
    pos_p = jnp.arange(x_prompt.shape[1])
    pos_s = page_table.shape[1] * PAGE_SIZE + jnp.arange(x_sample.shape[1])
    kp_l, vp_l, kip_l, poolp_l, convp_l = [], [], [], [], []
    ks_l, vs_l, kis_l, vns_l, pools_l, convs_l = [], [], [], [], [], []
    for l in range(DEPTH):
        lp = (w_mod[l], b_mod[l], g_attn_norm[l], g_mlp_norm[l], w_in[l], w_out[l], g_gmlp[l],
              w_spatial[l], b_spatial[l], w_pool[l], s_pool[l], w_dw[l], b_dw[l], g_conv_norm[l],
              b_conv_norm[l], w_up[l], w_down[l])
        pool0 = jnp.zeros((bp, 0, W_C), xp.dtype)
        conv0 = jnp.zeros((bp, CONV_STATE, W_D), xp.dtype)
        xp, (kp, vp, kip, _vnp, poolp, convp) = _layer(xp, c_prompt, pos_p, False, lp, None, pool0, conv0)
        xs, (ks, vs, kis, vns, pools, convs) = _layer(
            xs, c_sample, pos_s, True, lp, (cache_k[l], cache_v[l], cache_kidx[l], page_table),
            state_pool[l], state_conv[l])
        kp_l.append(kp); vp_l.append(vp); kip_l.append(kip); poolp_l.append(poolp); convp_l.append(convp)
        ks_l.append(ks); vs_l.append(vs); kis_l.append(kis); vns_l.append(vns); pools_l.append(pools); convs_l.append(convs)
    y_prompt = rmsnorm(xp, g_final)
    y_sample = rmsnorm(xs, g_final)
    return (y_prompt, y_sample,
            jnp.stack(kp_l), jnp.stack(vp_l), jnp.stack(kip_l), jnp.stack(poolp_l), jnp.stack(convp_l),
            jnp.stack(ks_l), jnp.stack(vs_l), jnp.stack(kis_l), jnp.stack(vns_l), jnp.stack(pools_l), jnp.stack(convs_l))
```

```python
import functools

import jax
import jax.numpy as jnp
import numpy as np
from jax import lax
from jax.experimental import pallas as pl
from jax.experimental.pallas import tpu as pltpu

D_MODEL = 1024
W_GRP = 256
HEAD_DIM = 64
A_HEADS = 4
ATT_HEADS = 4
IDX_HEADS = 8
IDX_DIM = 64
CHUNK = 128
TOPK_MAX = 256
PAGE_SIZE = 128
ROPE_THETA = 10000.0
POOL_WINDOWS = (2, 4, 8, 16)
POOL_STATE = 15
CONV_WIDTH = 31
CONV_STATE = 30
CONV_GROUPS = 4
D_FF = 4 * D_MODEL
EPS = 1e-6
IN_SIZES = (256, 256, 256, 256, 256, 512, 64, 8, 256, 256, 256)
D_IN = sum(IN_SIZES)
KIW_OFF = 1792
CX_OFF_RAW = 1864
D_IN_PAD = 2688
CX_OFF, DA_OFF, DG_OFF = 1920, 2176, 2432

LANES = 128
SUBLANES = 8
VMEM_LIMIT = 56 * 1024 * 1024

PRE_TM = 256
POST_TM = 512
FF_CHUNK = 1024
QB = 256
KT = 128
KT_UNROLL = 2
BISECT_ITERS = 32
POOL_HALO = 16
CONV_HALO = 32

F32 = jnp.float32
BF16 = jnp.bfloat16
NEG_INF = float("-inf")
KEY_NEG_INF = -2139095041
KEY_POS_INF = 2139095040


def _cparams(sem):
    return pltpu.CompilerParams(dimension_semantics=sem, vmem_limit_bytes=VMEM_LIMIT)


def _dot(a, b):
    return jnp.dot(a, b, preferred_element_type=F32)


def _dot_nt(a, b):
    return lax.dot_general(a, b, (((1,), (1,)), ((), ())), preferred_element_type=F32)


def _group_mean(x, gmat):
    hi = x.astype(BF16)
    lo = (x - hi.astype(F32)).astype(BF16)
    return _dot(hi, gmat) + _dot(lo, gmat)


def _rot_half(x):
    n = x.shape[-1]
    lane = lax.broadcasted_iota(jnp.int32, x.shape, x.ndim - 1)
    fwd = pltpu.roll(x, n - HEAD_DIM // 2, x.ndim - 1)
    bwd = pltpu.roll(x, HEAD_DIM // 2, x.ndim - 1)
    return jnp.where((lane % HEAD_DIM) < HEAD_DIM // 2, fwd, bwd)


def _rope(x, cos, sin_signed):
    reps = x.shape[-1] // LANES
    if reps > 1:
        cos = jnp.concatenate([cos] * reps, axis=-1)
        sin_signed = jnp.concatenate([sin_signed] * reps, axis=-1)
    return x * cos + _rot_half(x) * sin_signed


def _sigmoid(x):
    return 1.0 / (1.0 + jnp.exp(-x))


def _mod_kernel(c_ref, w_ref, b_ref, o_ref):
    c = c_ref[...]
    a = (c * _sigmoid(c)).astype(BF16)
    o_ref[...] = _dot(a, w_ref[...].astype(BF16)) + b_ref[...]


def _modulation(c_rows, w_mod_l, b_mod_l):
    n = c_rows.shape[0]
    ncol = w_mod_l.shape[1]
    tn = 1024
    return pl.pallas_call(
        _mod_kernel,
        out_shape=jax.ShapeDtypeStruct((n, ncol), F32),
        grid=(ncol // tn,),
        in_specs=[
            pl.BlockSpec((n, D_MODEL), lambda j: (0, 0)),
            pl.BlockSpec((D_MODEL, tn), lambda j: (0, j)),
            pl.BlockSpec((1, tn), lambda j: (0, j)),
        ],
        out_specs=pl.BlockSpec((n, tn), lambda j: (0, j)),
        compiler_params=_cparams(("arbitrary",)),
        name="adaln_mod",
    )(c_rows, w_mod_l, b_mod_l.reshape(1, ncol))


def _modulated_norm(x, g, shift, scale):
    ms = jnp.mean(x * x, axis=-1, keepdims=True)
    y = x * lax.rsqrt(ms + EPS) * g
    return y * (1.0 + scale) + shift


def _head_rmsnorm(av, g_row, gmat):
    ms = _group_mean(av * av, gmat)
    return av * lax.rsqrt(ms + EPS) * g_row


def _group_layernorm_silu(y, g_row, b_row, gmat):
    mu = _group_mean(y, gmat)
    d = y - mu
    var = _group_mean(d * d, gmat)
    yn = d * lax.rsqrt(var + EPS) * g_row + b_row
    return yn * _sigmoid(yn)


def _pre_prompt_kernel(x_ref, mod_ref, gattn_ref, win_ref, ggmlp_ref, wsp_ref, bsp_ref, cos_ref, sin_ref,
                       wpool_ref, spool_ref, wdw_ref, bdw_ref, gcn_ref, bcn_ref, gmat_ref,
                       acd_ref, q_ref, qi_ref, kbf_ref, vbf_ref, ki2_ref, widx_ref,
                       k_ref, v_ref, ki_ref, poolst_ref, convst_ref,
                       pool_buf, conv_buf):
    i = pl.program_id(1)
    tm = x_ref.shape[1]
    gmat = gmat_ref[...]

    @pl.when(i == 0)
    def _():
        pool_buf[0:POOL_HALO, :] = jnp.zeros((POOL_HALO, W_GRP), F32)
        conv_buf[0:CONV_HALO, :] = jnp.zeros((CONV_HALO, W_GRP), F32)

    mod = mod_ref[0]
    h = _modulated_norm(x_ref[0], gattn_ref[...], mod[0:1], mod[1:2]).astype(BF16)

    def proj(off, width):
        return _dot(h, win_ref[:, off:off + width])

    a_u = proj(0, 256)
    v_n = _head_rmsnorm(proj(256, 256), ggmlp_ref[...], gmat)
    v_bf = v_n.astype(BF16)
    lane256 = lax.broadcasted_iota(jnp.int32, (CHUNK, 256), 1)
    rr = lax.broadcasted_iota(jnp.int32, (CHUNK, CHUNK), 0)
    cc = lax.broadcasted_iota(jnp.int32, (CHUNK, CHUNK), 1)
    ws = [jnp.where(cc <= rr, wsp_ref[hh], 0.0).astype(BF16) for hh in range(A_HEADS)]
    gate_chunks = []
    for c in range(tm // CHUNK):
        vc = v_bf[c * CHUNK:(c + 1) * CHUNK, :]
        s = [_dot(ws[hh], vc) for hh in range(A_HEADS)]
        hid = lane256 // HEAD_DIM
        g = jnp.where(hid == 0, s[0], jnp.where(hid == 1, s[1], jnp.where(hid == 2, s[2], s[3])))
        gate_chunks.append(g + bsp_ref[...])
    out_a = a_u * jnp.concatenate(gate_chunks, axis=0)

    cos = cos_ref[...]
    sin = sin_ref[...]
    q = _rope(proj(512, 256), cos, sin) * (HEAD_DIM ** -0.5)
    q_ref[0] = q.astype(BF16)
    k = _rope(proj(768, 256), cos, sin)
    k_ref[0] = k
    kbf_ref[0] = k.astype(BF16)
    v = proj(1024, 256)
    v_ref[0] = v
    vbf_ref[0] = v.astype(BF16)
    qi_ref[0] = _rope(proj(1280, 512), cos, sin).astype(BF16)
    kiw = proj(KIW_OFF, LANES)
    ki_full = _rope(kiw, cos, sin)
    ki = ki_full[:, 0:IDX_DIM]
    ki_ref[0] = ki
    ki_b = ki.astype(BF16)
    ki2_ref[0] = jnp.concatenate([ki_b, ki_b], axis=-1)
    widx_ref[0] = kiw[:, IDX_DIM:IDX_DIM + IDX_HEADS] * ((IDX_HEADS * IDX_DIM) ** -0.5)

    c_x = proj(CX_OFF, 256)
    pool_buf[POOL_HALO:POOL_HALO + tm, :] = c_x
    lane128 = lax.broadcasted_iota(jnp.int32, (tm, LANES), 1)
    tpos = i * tm + lax.broadcasted_iota(jnp.int32, (tm, LANES), 0)
    upper = lane128 >= HEAD_DIM
    pooled = []
    for col, (w_lo, w_hi) in enumerate(((2, 4), (8, 16))):
        cs = slice(col * LANES, (col + 1) * LANES)
        s_lo = pool_buf[POOL_HALO:POOL_HALO + tm, cs]
        for j in range(1, w_lo):
            s_lo = s_lo + pool_buf[POOL_HALO - j:POOL_HALO - j + tm, cs]
        s_hi = pool_buf[POOL_HALO - w_lo:POOL_HALO - w_lo + tm, cs]
        for j in range(w_lo + 1, w_hi):
            s_hi = s_hi + pool_buf[POOL_HALO - j:POOL_HALO - j + tm, cs]
        win = jnp.where(upper, w_hi, w_lo)
        cnt = jnp.minimum(win, tpos + 1).astype(F32)
        pooled.append(jnp.where(upper, s_lo + s_hi, s_lo) / cnt)
    diff = (jnp.concatenate(pooled, axis=-1) - c_x).astype(BF16)
    out_c = _dot(diff, wpool_ref[...]) * spool_ref[...]
    poolst_ref[0] = pool_buf[tm:tm + POOL_HALO, :]
    pool_buf[0:POOL_HALO, :] = pool_buf[tm:tm + POOL_HALO, :]

    glu = proj(DA_OFF, 256) * _sigmoid(proj(DG_OFF, 256))
    conv_buf[CONV_HALO:CONV_HALO + tm, :] = glu
    base = CONV_HALO - CONV_STATE
    y = conv_buf[base:base + tm, :] * wdw_ref[0:1, :]
    for j in range(1, CONV_WIDTH):
        y = y + conv_buf[base + j:base + j + tm, :] * wdw_ref[j:j + 1, :]
    y = y + bdw_ref[...]
    out_d = _group_layernorm_silu(y, gcn_ref[...], bcn_ref[...], gmat)
    convst_ref[0] = conv_buf[tm:tm + CONV_HALO, :]
    conv_buf[0:CONV_HALO, :] = conv_buf[tm:tm + CONV_HALO, :]

    acd_ref[0] = jnp.concatenate([out_a, out_c, out_d], axis=-1).astype(BF16)


def _pre_prompt(x, mod, lw, tabs):
    B, S, _ = x.shape
    tm = min(PRE_TM, S)
    n_t = S // tm
    tok = lambda w: pl.BlockSpec((1, tm, w), lambda b, i: (b, i, 0))
    const = lambda shape: pl.BlockSpec(shape, lambda b, i: tuple(0 for _ in shape))
    in_specs = [
        tok(D_MODEL),
        pl.BlockSpec((1, 6, D_MODEL), lambda b, i: (b, 0, 0)),
        const((1, D_MODEL)),
        const((D_MODEL, D_IN_PAD)),
        const((1, W_GRP)),
        const((A_HEADS, CHUNK, CHUNK)),
        const((CHUNK, W_GRP)),
        pl.BlockSpec((tm, LANES), lambda b, i: (i, 0)),
        pl.BlockSpec((tm, LANES), lambda b, i: (i, 0)),
        const((W_GRP, W_GRP)),
        const((1, W_GRP)),
        const((32, W_GRP)),
        const((1, W_GRP)),
        const((1, W_GRP)),
        const((1, W_GRP)),
        const((W_GRP, W_GRP)),
    ]
    out_shapes = [
        jax.ShapeDtypeStruct((B, S, 768), BF16),
        jax.ShapeDtypeStruct((B, S, 256), BF16),
        jax.ShapeDtypeStruct((B, S, 512), BF16),
        jax.ShapeDtypeStruct((B, S, 256), BF16),
        jax.ShapeDtypeStruct((B, S, 256), BF16),
        jax.ShapeDtypeStruct((B, S, LANES), BF16),
        jax.ShapeDtypeStruct((B, S, IDX_HEADS), F32),
        jax.ShapeDtypeStruct((B, S, 256), F32),
        jax.ShapeDtypeStruct((B, S, 256), F32),
        jax.ShapeDtypeStruct((B, S, IDX_DIM), F32),
        jax.ShapeDtypeStruct((B, POOL_HALO, W_GRP), F32),
        jax.ShapeDtypeStruct((B, CONV_HALO, W_GRP), F32),
    ]
    out_specs = [tok(768), tok(256), tok(512), tok(256), tok(256), tok(LANES), tok(IDX_HEADS),
                 tok(256), tok(256), tok(IDX_DIM),
                 pl.BlockSpec((1, POOL_HALO, W_GRP), lambda b, i: (b, 0, 0)),
                 pl.BlockSpec((1, CONV_HALO, W_GRP), lambda b, i: (b, 0, 0))]
    return pl.pallas_call(
        _pre_prompt_kernel,
        out_shape=out_shapes,
        grid=(B, n_t),
        in_specs=in_specs,
        out_specs=out_specs,
        scratch_shapes=[pltpu.VMEM((POOL_HALO + tm, W_GRP), F32), pltpu.VMEM((CONV_HALO + tm, W_GRP), F32)],
        compiler_params=_cparams(("arbitrary", "arbitrary")),
        name="pre_prompt",
    )(x, mod, lw["g_attn"], lw["w_in"], lw["g_gmlp"], lw["w_spatial"], lw["b_spatial_t"], tabs["cos"], tabs["sin"],
      lw["w_pool_bd"], lw["s_pool"], lw["w_dw"], lw["b_dw"], lw["g_cn"], lw["b_cn"], tabs["gmat"])


def _key_to_float(key):
    bits = jnp.where(key >= 0, key, key ^ jnp.int32(0x7FFFFFFF))
    return pltpu.bitcast(bits, F32)


def _dsa_prompt_kernel(q_ref, qi_ref, w_ref, ki2_ref, k_ref, v_ref, o_ref, sc_ref, lg_ref, wb_ref, *, topk):
    i = pl.program_id(1)
    qb = q_ref.shape[1]
    n_rg = qb // KT
    tiles_per_blk = qb // KT
    n_it = ((i + 1) * tiles_per_blk + KT_UNROLL - 1) // KT_UNROLL
    lane = lax.broadcasted_iota(jnp.int32, (KT, KT), 1)
    row = lax.broadcasted_iota(jnp.int32, (KT, KT), 0)
    low_half = lane < HEAD_DIM

    for g in range(n_rg):
        w = w_ref[0, g * KT:(g + 1) * KT, :]
        for hh in range(IDX_HEADS):
            wb_ref[g, hh] = jnp.broadcast_to(w[:, hh:hh + 1], (KT, KT))

    lhs = []
    for g in range(n_rg):
        per_head = []
        for p in range(IDX_HEADS // 2):
            slab = qi_ref[0, g * KT:(g + 1) * KT, p * LANES:(p + 1) * LANES]
            zero = jnp.zeros_like(slab)
            per_head.append(jnp.where(low_half, slab, zero))
            per_head.append(jnp.where(low_half, zero, slab))
        lhs.append(per_head)

    def score_body(it, carry):
        for u in range(KT_UNROLL):
            kt = it * KT_UNROLL + u
            kc = ki2_ref[0, pl.ds(pl.multiple_of(kt * KT, KT), KT), :]
            for g in range(n_rg):
                acc = jnp.zeros((KT, KT), F32)
                for hh in range(IDX_HEADS):
                    acc = acc + wb_ref[g, hh] * jnp.maximum(_dot_nt(lhs[g][hh], kc), 0.0)
                q_pos = (i * tiles_per_blk + g) * KT + row
                sc_ref[g, kt] = jnp.where(kt * KT + lane <= q_pos, acc, NEG_INF)
        return carry

    lax.fori_loop(0, n_it, score_body, 0)

    def count_ge(thr):
        def body(it, accs):
            accs = list(accs)
            for u in range(KT_UNROLL):
                kt = it * KT_UNROLL + u
                for g in range(n_rg):
                    accs[g] = accs[g] + jnp.where(sc_ref[g, kt] >= thr[g], 1.0, 0.0)
            return tuple(accs)
        accs = lax.fori_loop(0, n_it, body, tuple(jnp.zeros((KT, KT), F32) for _ in range(n_rg)))
        return [jnp.sum(a, axis=1, keepdims=True) for a in accs]

    colrow = lax.broadcasted_iota(jnp.int32, (KT, 1), 0)
    k_row = [jnp.minimum(topk, (i * tiles_per_blk + g) * KT + colrow + 1).astype(F32) for g in range(n_rg)]

    def bisect_body(_, carry):
        los, his = carry
        mids = [(lo >> 1) + (hi >> 1) + (lo & hi & 1) for lo, hi in zip(los, his)]
        cnts = count_ge([_key_to_float(m) for m in mids])
        ge = [c >= kr for c, kr in zip(cnts, k_row)]
        return (tuple(jnp.where(c, m, lo) for c, m, lo in zip(ge, mids, los)),
                tuple(jnp.where(c, hi, m) for c, m, hi in zip(ge, mids, his)))

    lo0 = tuple(jnp.full((KT, 1), KEY_NEG_INF + 1, jnp.int32) for _ in range(n_rg))
    hi0 = tuple(jnp.full((KT, 1), KEY_POS_INF + 1, jnp.int32) for _ in range(n_rg))
    los, _ = lax.fori_loop(0, BISECT_ITERS, bisect_body, (lo0, hi0))
    thr = [_key_to_float(lo) for lo in los]

    def count_sel(cut):
        def body(it, accs):
            accs = list(accs)
            for u in range(KT_UNROLL):
                kt = it * KT_UNROLL + u
                for g in range(n_rg):
                    s = sc_ref[g, kt]
                    sel = (s > thr[g]) | ((s == thr[g]) & (kt * KT + lane <= cut[g]))
                    accs[g] = accs[g] + jnp.where(sel, 1.0, 0.0)
            return tuple(accs)
        accs = lax.fori_loop(0, n_it, body, tuple(jnp.zeros((KT, KT), F32) for _ in range(n_rg)))
        return [jnp.sum(a, axis=1, keepdims=True) for a in accs]

    n_keys = n_it * KT_UNROLL * KT
    n_ge = count_ge(thr)
    excess = n_ge[0] > k_row[0]
    for g in range(1, n_rg):
        excess = excess | (n_ge[g] > k_row[g])
    any_excess = jnp.max(jnp.where(excess, 1.0, 0.0)) > 0.0

    def tie_cut(_):
        def body(_, carry):
            los_c, his_c = carry
            mids = [(lo + hi) >> 1 for lo, hi in zip(los_c, his_c)]
            cnts = count_sel(mids)
            ge = [c >= kr for c, kr in zip(cnts, k_row)]
            return (tuple(jnp.where(c, lo, m) for c, m, lo in zip(ge, mids, los_c)),
                    tuple(jnp.where(c, m, hi) for c, m, hi in zip(ge, mids, his_c)))
        lo_c = tuple(jnp.full((KT, 1), -1, jnp.int32) for _ in range(n_rg))
        hi_c = tuple(jnp.full((KT, 1), 1, jnp.int32) * (n_keys - 1) for _ in range(n_rg))
        _, his_c = lax.fori_loop(0, 13, body, (lo_c, hi_c))
        return his_c

    def no_tie_cut(_):
        return tuple(jnp.full((KT, 1), 1, jnp.int32) * n_keys for _ in range(n_rg))

    cut = lax.cond(any_excess, tie_cut, no_tie_cut, 0)

    def bias_body(it, carry):
        for u in range(KT_UNROLL):
            kt = it * KT_UNROLL + u
            for g in range(n_rg):
                s = sc_ref[g, kt]
                sel = (s > thr[g]) | ((s == thr[g]) & (kt * KT + lane <= cut[g]))
                sc_ref[g, kt] = jnp.where(sel, 0.0, NEG_INF)
        return carry

    lax.fori_loop(0, n_it, bias_body, 0)

    for p in range(ATT_HEADS // 2):
        cs = slice(p * LANES, (p + 1) * LANES)
        outs = [[None, None] for _ in range(n_rg)]
        for half in range(2):
            qh = []
            for g in range(n_rg):
                slab = q_ref[0, g * KT:(g + 1) * KT, cs]
                zero = jnp.zeros_like(slab)
                qh.append(jnp.where(low_half, slab, zero) if half == 0 else jnp.where(low_half, zero, slab))

            def logit_body(it, mruns):
                mruns = list(mruns)
                for u in range(KT_UNROLL):
                    kt = it * KT_UNROLL + u
                    kc = k_ref[0, pl.ds(pl.multiple_of(kt * KT, KT), KT), cs]
                    for g in range(n_rg):
                        lgt = _dot_nt(qh[g], kc) + sc_ref[g, kt]
                        lg_ref[g, kt] = lgt
                        mruns[g] = jnp.maximum(mruns[g], lgt)
                return tuple(mruns)

            mruns = lax.fori_loop(0, n_it, logit_body, tuple(jnp.full((KT, KT), NEG_INF, F32) for _ in range(n_rg)))
            mx = [jnp.max(m, axis=1, keepdims=True) for m in mruns]

            def pv_body(it, carry):
                lruns, accs = carry
                lruns, accs = list(lruns), list(accs)
                for u in range(KT_UNROLL):
                    kt = it * KT_UNROLL + u
                    vc = v_ref[0, pl.ds(pl.multiple_of(kt * KT, KT), KT), cs]
                    for g in range(n_rg):
                        pe = jnp.exp(lg_ref[g, kt] - mx[g])
                        lruns[g] = lruns[g] + pe
                        accs[g] = accs[g] + _dot(pe.astype(BF16), vc)
                return tuple(lruns), tuple(accs)

            zeros = tuple(jnp.zeros((KT, KT), F32) for _ in range(n_rg))
            lruns, accs = lax.fori_loop(0, n_it, pv_body, (zeros, zeros))
            for g in range(n_rg):
                outs[g][half] = accs[g] / jnp.sum(lruns[g], axis=1, keepdims=True)
        for g in range(n_rg):
            o_ref[0, g * KT:(g + 1) * KT, cs] = jnp.where(low_half, outs[g][0], outs[g][1]).astype(BF16)


def _dsa_prompt(q, qi, widx, ki2, kbf, vbf):
    B, S, _ = q.shape
    qb = min(QB, S)
    n_q = S // qb
    n_rg = qb // KT
    n_kt = S // KT + KT_UNROLL
    blk = lambda w: pl.BlockSpec((1, qb, w), lambda b, i: (b, i, 0))
    full = lambda w: pl.BlockSpec((1, S, w), lambda b, i: (b, 0, 0))
    return pl.pallas_call(
        functools.partial(_dsa_prompt_kernel, topk=min(TOPK_MAX, S // 4)),
        out_shape=jax.ShapeDtypeStruct((B, S, 256), BF16),
        grid=(B, n_q),
        in_specs=[blk(256), blk(512), blk(IDX_HEADS), full(LANES), full(256), full(256)],
        out_specs=blk(256),
        scratch_shapes=[pltpu.VMEM((n_rg, n_kt, KT, KT), F32), pltpu.VMEM((n_rg, n_kt, KT, KT), F32),
                        pltpu.VMEM((n_rg, IDX_HEADS, KT, KT), F32)],
        compiler_params=_cparams(("arbitrary", "arbitrary")),
        name="dsa_prompt",
    )(q, qi, widx, ki2, kbf, vbf)


def _post_kernel(x_ref, acd_ref, ob_ref, mod_ref, gmlp_ref, woacd_ref, wob_ref, wup_ref, wdn_ref, gfin_ref, o_ref,
                 *, final):
    mod = mod_ref[0]
    mixed = _dot(acd_ref[0], woacd_ref[...]) + _dot(ob_ref[0], wob_ref[...])
    x1 = x_ref[0] + mod[2] * mixed
    h2 = _modulated_norm(x1, gmlp_ref[...], mod[3], mod[4]).astype(BF16)
    acc = jnp.zeros_like(x1)
    for c in range(D_FF // FF_CHUNK):
        u = jnp.maximum(_dot(h2, wup_ref[:, c * FF_CHUNK:(c + 1) * FF_CHUNK]), 0.0)
        acc = acc + _dot((u * u).astype(BF16), wdn_ref[c * FF_CHUNK:(c + 1) * FF_CHUNK, :])
    x2 = x1 + mod[5] * acc
    if final:
        ms = jnp.mean(x2 * x2, axis=-1, keepdims=True)
        x2 = x2 * lax.rsqrt(ms + EPS) * gfin_ref[...]
    o_ref[0] = x2


def _post(x, acd, ob, mod, lw, g_final, final):
    G, T, _ = x.shape
    tm = min(POST_TM, T)
    R = mod.shape[2]
    tok = lambda w: pl.BlockSpec((1, tm, w), lambda b, i: (b, i, 0))
    const = lambda shape: pl.BlockSpec(shape, lambda b, i: tuple(0 for _ in shape), pipeline_mode=pl.Buffered(1))
    if R == 1:
        mod_spec = pl.BlockSpec((1, 6, 1, D_MODEL), lambda b, i: (b, 0, 0, 0))
    else:
        mod_spec = pl.BlockSpec((1, 6, tm, D_MODEL), lambda b, i: (b, 0, i, 0))
    return pl.pallas_call(
        functools.partial(_post_kernel, final=final),
        out_shape=jax.ShapeDtypeStruct((G, T, D_MODEL), F32),
        grid=(G, T // tm),
        in_specs=[tok(D_MODEL), tok(768), tok(256), mod_spec, const((1, D_MODEL)),
                  const((768, D_MODEL)), const((256, D_MODEL)), const((D_MODEL, D_FF)), const((D_FF, D_MODEL)),
                  const((1, D_MODEL))],
        out_specs=tok(D_MODEL),
        compiler_params=_cparams(("arbitrary", "arbitrary")),
        name="post_final" if final else "post",
    )(x, acd, ob, mod, lw["g_mlp"], lw["w_out_acd"], lw["w_out_b"], lw["w_up"], lw["w_down"], g_final)


def _layer_weights(l, w_mod, b_mod, g_attn_norm, g_mlp_norm, w_in, w_out, g_gmlp, w_spatial, b_spatial,
                   w_pool, s_pool, w_dw, b_dw, g_conv_norm, b_conv_norm, w_up, w_down):
    w_in_l = w_in[l]
    w_in_pad = jnp.concatenate(
        [w_in_l[:, :CX_OFF_RAW], jnp.zeros((D_MODEL, CX_OFF - CX_OFF_RAW), w_in_l.dtype), w_in_l[:, CX_OFF_RAW:]],
        axis=1).astype(BF16)
    w_pool_bd = jnp.zeros((W_GRP, W_GRP), F32)
    for g in range(len(POOL_WINDOWS)):
        w_pool_bd = w_pool_bd.at[g * 64:(g + 1) * 64, g * 64:(g + 1) * 64].set(w_pool[l, g])
    w_out_l = w_out[l]
    return dict(
        w_mod=w_mod[l], b_mod=b_mod[l],
        g_attn=g_attn_norm[l].reshape(1, D_MODEL), g_mlp=g_mlp_norm[l].reshape(1, D_MODEL),
        w_in=w_in_pad,
        g_gmlp=g_gmlp[l].reshape(1, W_GRP),
        w_spatial=w_spatial[l],
        b_spatial_t=jnp.repeat(b_spatial[l].T, HEAD_DIM, axis=1),
        w_sp0=jnp.repeat(w_spatial[l, :, 0, 0], HEAD_DIM).reshape(1, W_GRP),
        b_sp0=jnp.repeat(b_spatial[l, :, 0], HEAD_DIM).reshape(1, W_GRP),
        w_pool_bd=w_pool_bd.astype(BF16), s_pool=s_pool[l].reshape(1, W_GRP),
        w_dw=jnp.concatenate([w_dw[l], jnp.zeros((1, W_GRP), F32)], axis=0), b_dw=b_dw[l].reshape(1, W_GRP),
        g_cn=g_conv_norm[l].reshape(1, W_GRP), b_cn=b_conv_norm[l].reshape(1, W_GRP),
        w_out_acd=jnp.concatenate([w_out_l[0:256], w_out_l[512:1024]], axis=0).astype(BF16),
        w_out_b=w_out_l[256:512].astype(BF16),
        w_up=w_up[l].astype(BF16), w_down=w_down[l].astype(BF16),
    )


def _tables(pos):
    half = HEAD_DIM // 2
    freqs = ROPE_THETA ** (-jnp.arange(half, dtype=F32) / half)
    ang = pos.astype(F32)[:, None] * freqs[None, :]
    cos = jnp.tile(jnp.cos(ang), (1, LANES // half))
    sin = jnp.sin(ang)
    sin_signed = jnp.tile(jnp.concatenate([-sin, sin], axis=1), (1, LANES // HEAD_DIM))
    gid = jnp.arange(W_GRP) // HEAD_DIM
    gmat = jnp.where(gid[:, None] == gid[None, :], 1.0 / HEAD_DIM, 0.0).astype(BF16)
    return dict(cos=cos, sin=sin_signed, gmat=gmat)


def _prompt_layer(x, mod_p, lw, tabs, g_final, final):
    B, S, _ = x.shape
    (acd, q, qi, kbf, vbf, ki2, widx, k, v, ki, poolst, convst) = _pre_prompt(x, mod_p, lw, tabs)
    ob = _dsa_prompt(q, qi, widx, ki2, kbf, vbf)
    x_new = _post(x, acd, ob, mod_p.reshape(B, 6, 1, D_MODEL), lw, g_final, final)
    return x_new, (k.reshape(B, S, ATT_HEADS, HEAD_DIM), v.reshape(B, S, ATT_HEADS, HEAD_DIM), ki,
                   poolst[:, POOL_HALO - POOL_STATE:], convst[:, CONV_HALO - CONV_STATE:])


def _pre_sample_kernel(x_ref, mod_ref, gattn_ref, win_ref, ggmlp_ref, wsp0_ref, bsp0_ref, cos_ref, sin_ref,
                       wpool_ref, spool_ref, wdw_ref, bdw_ref, gcn_ref, bcn_ref, gmat_ref, poolprev_ref, convprev_ref,
                       acd_ref, q_ref, qi_ref, widx_ref, k_ref, v_ref, ki_ref, vn_ref, poolnew_ref, convnew_ref):
    gmat = gmat_ref[...]
    h = _modulated_norm(x_ref[...], gattn_ref[...], mod_ref[0], mod_ref[1]).astype(BF16)

    def proj(off, width):
        return _dot(h, win_ref[:, off:off + width])

    a_u = proj(0, 256)
    v_n = _head_rmsnorm(proj(256, 256), ggmlp_ref[...], gmat)
    vn_ref[...] = v_n
    out_a = a_u * (wsp0_ref[...] * v_n + bsp0_ref[...])

    cos = cos_ref[...]
    sin = sin_ref[...]
    q_ref[...] = _rope(proj(512, 256), cos, sin) * (HEAD_DIM ** -0.5)
    k_ref[...] = _rope(proj(768, 256), cos, sin)
    v_ref[...] = proj(1024, 256)
    qi_ref[...] = _rope(proj(1280, 512), cos, sin).astype(BF16)
    kiw = proj(KIW_OFF, LANES)
    ki_ref[...] = _rope(kiw, cos, sin)[:, 0:IDX_DIM]
    widx_ref[...] = kiw[:, IDX_DIM:IDX_DIM + IDX_HEADS] * ((IDX_HEADS * IDX_DIM) ** -0.5)

    c_x = proj(CX_OFF, 256)
    sums = {}
    run = c_x
    for j in range(1, max(POOL_WINDOWS)):
        run = run + poolprev_ref[POOL_STATE - j]
        if j + 1 in POOL_WINDOWS:
            sums[j + 1] = run * (1.0 / (j + 1))
    gid = lax.broadcasted_iota(jnp.int32, c_x.shape, 1) // HEAD_DIM
    pooled = jnp.where(gid == 0, sums[2], jnp.where(gid == 1, sums[4], jnp.where(gid == 2, sums[8], sums[16])))
    diff = (pooled - c_x).astype(BF16)
    out_c = _dot(diff, wpool_ref[...]) * spool_ref[...]
    for r in range(POOL_STATE - 1):
        poolnew_ref[r] = poolprev_ref[r + 1]
    poolnew_ref[POOL_STATE - 1] = c_x

    glu = proj(DA_OFF, 256) * _sigmoid(proj(DG_OFF, 256))
    y = glu * wdw_ref[CONV_STATE:CONV_STATE + 1, :]
    for j in range(CONV_STATE):
        y = y + convprev_ref[j] * wdw_ref[j:j + 1, :]
    y = y + bdw_ref[...]
    out_d = _group_layernorm_silu(y, gcn_ref[...], bcn_ref[...], gmat)
    for r in range(CONV_STATE - 1):
        convnew_ref[r] = convprev_ref[r + 1]
    convnew_ref[CONV_STATE - 1] = glu

    acd_ref[...] = jnp.concatenate([out_a, out_c, out_d], axis=-1).astype(BF16)


def _pre_sample(x, mod_s, lw, tabs_s, pool_prev_t, conv_prev_t):
    n = x.shape[0]
    out_shapes = [
        jax.ShapeDtypeStruct((n, 768), BF16),
        jax.ShapeDtypeStruct((n, 256), F32),
        jax.ShapeDtypeStruct((n, 512), BF16),
        jax.ShapeDtypeStruct((n, IDX_HEADS), F32),
        jax.ShapeDtypeStruct((n, 256), F32),
        jax.ShapeDtypeStruct((n, 256), F32),
        jax.ShapeDtypeStruct((n, IDX_DIM), F32),
        jax.ShapeDtypeStruct((n, 256), F32),
        jax.ShapeDtypeStruct((POOL_STATE, n, W_GRP), F32),
        jax.ShapeDtypeStruct((CONV_STATE, n, W_GRP), F32),
    ]
    return pl.pallas_call(
        _pre_sample_kernel,
        out_shape=out_shapes,
        compiler_params=pltpu.CompilerParams(vmem_limit_bytes=VMEM_LIMIT),
        name="pre_sample",
    )(x, mod_s, lw["g_attn"], lw["w_in"], lw["g_gmlp"], lw["w_sp0"], lw["b_sp0"], tabs_s["cos"], tabs_s["sin"],
      lw["w_pool_bd"], lw["s_pool"], lw["w_dw"], lw["b_dw"], lw["g_cn"], lw["b_cn"], tabs_s["gmat"],
      pool_prev_t, conv_prev_t)


SC_CHUNKS = 18
PAGES_PER_CHUNK = 8
NEW_CHUNK = 16


def _idx_sample_kernel(pt_ref, qi_ref, w_ref, kinew_ref, cki_ref, sc_ref, kbuf, sems, *, layer):
    b = pl.program_id(0)
    nb = pl.num_programs(0)
    n_pages = kbuf.shape[1]

    def page_copy(bb, p, slot):
        return pltpu.make_async_copy(cki_ref.at[layer, pt_ref[bb, p]], kbuf.at[slot, p], sems.at[slot])

    def start_all(bb, slot):
        def body(p, c):
            page_copy(bb, p, slot).start()
            return c
        lax.fori_loop(0, n_pages, body, 0)

    slot = b % 2

    @pl.when(b == 0)
    def _():
        start_all(0, 0)

    @pl.when(b + 1 < nb)
    def _():
        start_all(b + 1, 1 - slot)

    def wait_body(p, c):
        page_copy(b, p, slot).wait()
        return c
    lax.fori_loop(0, n_pages, wait_body, 0)

    qi = qi_ref[0]
    w = w_ref[0]

    def page_scores(keys_bf):
        d = jnp.maximum(_dot_nt(qi, keys_bf), 0.0) * w
        return jnp.sum(d, axis=0, keepdims=True)

    def chunk_body(c, carry):
        for j in range(0, PAGES_PER_CHUNK, 2):
            p0 = c * PAGES_PER_CHUNK + j
            keys = kbuf[slot, pl.ds(p0, 2)].reshape(2 * PAGE_SIZE, IDX_DIM).astype(BF16)
            s = page_scores(keys)
            sc_ref[0, c, j:j + 1, :] = s[:, 0:PAGE_SIZE]
            sc_ref[0, c, j + 1:j + 2, :] = s[:, PAGE_SIZE:2 * PAGE_SIZE]
        return carry

    lax.fori_loop(0, n_pages // PAGES_PER_CHUNK, chunk_body, 0)

    rowi = lax.broadcasted_iota(jnp.int32, (PAGE_SIZE, IDX_DIM), 0)
    knew = jnp.where(rowi == 0, jnp.broadcast_to(kinew_ref[0], (PAGE_SIZE, IDX_DIM)), 0.0).astype(BF16)
    s_new = page_scores(knew)
    lane = lax.broadcasted_iota(jnp.int32, (1, PAGE_SIZE), 1)
    neg = jnp.full((SUBLANES, PAGE_SIZE), NEG_INF, F32)
    sc_ref[0, NEW_CHUNK] = neg
    sc_ref[0, NEW_CHUNK, 0:1, :] = jnp.where(lane == 0, s_new, NEG_INF)
    sc_ref[0, NEW_CHUNK + 1] = neg


def _idx_sample(page_table, qi3, w3, ki_new3, cki, layer):
    n, n_pages = page_table.shape
    assert n_pages == (NEW_CHUNK * PAGES_PER_CHUNK)
    grid_spec = pltpu.PrefetchScalarGridSpec(
        num_scalar_prefetch=1,
        grid=(n,),
        in_specs=[
            pl.BlockSpec((1, IDX_HEADS, IDX_DIM), lambda b, pt: (b, 0, 0)),
            pl.BlockSpec((1, IDX_HEADS, 1), lambda b, pt: (b, 0, 0)),
            pl.BlockSpec((1, 1, IDX_DIM), lambda b, pt: (b, 0, 0)),
            pl.BlockSpec(memory_space=pl.ANY),
        ],
        out_specs=pl.BlockSpec((1, SC_CHUNKS, SUBLANES, PAGE_SIZE), lambda b, pt: (b, 0, 0, 0)),
        scratch_shapes=[pltpu.VMEM((2, n_pages, PAGE_SIZE, IDX_DIM), F32), pltpu.SemaphoreType.DMA((2,))],
    )
    return pl.pallas_call(
        functools.partial(_idx_sample_kernel, layer=layer),
        out_shape=jax.ShapeDtypeStruct((n, SC_CHUNKS, SUBLANES, PAGE_SIZE), F32),
        grid_spec=grid_spec,
        compiler_params=_cparams(("arbitrary",)),
        name="idx_sample",
    )(page_table, qi3, w3, ki_new3, cki)


def _topk_sample_kernel(sc_ref, pt_ref, idx_ref, rank_buf, phys_buf, thr_buf, cut_buf, *, topk):
    n = sc_ref.shape[0]
    rows = SC_CHUNKS * SUBLANES
    shape4 = sc_ref.shape

    def total(x):
        x = jnp.sum(x, axis=1, keepdims=True)
        x = jnp.sum(x, axis=2, keepdims=True)
        return jnp.sum(x, axis=3, keepdims=True)

    kf = float(topk)

    def bisect_body(_, carry):
        lo, hi = carry
        mid = (lo >> 1) + (hi >> 1) + (lo & hi & 1)
        cnt = total(jnp.where(sc_ref[...] >= _key_to_float(mid), 1.0, 0.0))
        ge = cnt >= kf
        return jnp.where(ge, mid, lo), jnp.where(ge, hi, mid)

    lo0 = jnp.full((n, 1, 1, 1), KEY_NEG_INF + 1, jnp.int32)
    hi0 = jnp.full((n, 1, 1, 1), KEY_POS_INF + 1, jnp.int32)
    lo, _ = lax.fori_loop(0, BISECT_ITERS, bisect_body, (lo0, hi0))
    thr = _key_to_float(lo)

    kidx = (lax.broadcasted_iota(jnp.int32, shape4, 1) * (SUBLANES * PAGE_SIZE)
            + lax.broadcasted_iota(jnp.int32, shape4, 2) * PAGE_SIZE
            + lax.broadcasted_iota(jnp.int32, shape4, 3))

    def count_sel(cut):
        s = sc_ref[...]
        sel = (s > thr) | ((s == thr) & (kidx <= cut))
        return total(jnp.where(sel, 1.0, 0.0))

    n_keys = SC_CHUNKS * SUBLANES * PAGE_SIZE

    def cut_body(_, carry):
        lo_c, hi_c = carry
        mid = (lo_c + hi_c) >> 1
        ge = count_sel(mid) >= kf
        return jnp.where(ge, lo_c, mid), jnp.where(ge, mid, hi_c)

    lo_c = jnp.full((n, 1, 1, 1), -1, jnp.int32)
    hi_c = jnp.full((n, 1, 1, 1), n_keys - 1, jnp.int32)
    _, cut = lax.fori_loop(0, 15, cut_body, (lo_c, hi_c))
    thr_buf[...] = thr
    cut_buf[...] = cut

    upper = (lax.broadcasted_iota(jnp.int32, (PAGE_SIZE, PAGE_SIZE), 0)
             < lax.broadcasted_iota(jnp.int32, (PAGE_SIZE, PAGE_SIZE), 1)).astype(BF16)
    ones = jnp.ones((PAGE_SIZE, PAGE_SIZE), BF16)
    lower = (lax.broadcasted_iota(jnp.int32, (rows, rows), 1)
             < lax.broadcasted_iota(jnp.int32, (rows, rows), 0)).astype(BF16)
    lane2 = lax.broadcasted_iota(jnp.int32, (rows, PAGE_SIZE), 1)
    row2 = lax.broadcasted_iota(jnp.int32, (rows, PAGE_SIZE), 0)
    slot_iota = lax.broadcasted_iota(jnp.int32, (topk, PAGE_SIZE), 0).astype(F32)

    def seq_body(b, carry):
        s = sc_ref[b].reshape(rows, PAGE_SIZE)
        t = thr_buf[b, 0]
        c = cut_buf[b, 0]
        sel = (s > t) | ((s == t) & (row2 * PAGE_SIZE + lane2 <= c))
        m = jnp.where(sel, 1.0, 0.0).astype(BF16)
        within = _dot(m, upper)
        tot = _dot(m, ones)
        offs = _dot(lower, tot.astype(BF16))
        rank_buf[...] = jnp.where(sel, within + offs, -1.0)
        page = pt_ref[b]
        n_pages = page.shape[0]
        lane_p = lax.broadcasted_iota(jnp.int32, (n_pages, PAGE_SIZE), 1)
        phys_buf[0:n_pages, :] = (page * PAGE_SIZE + lane_p).astype(F32)
        phys_buf[n_pages:rows, :] = jnp.full((rows - n_pages, PAGE_SIZE), -1.0, F32)

        def row_body(r, acc):
            hit = rank_buf[pl.ds(r, 1), :] == slot_iota
            return acc + jnp.where(hit, phys_buf[pl.ds(r, 1), :], 0.0)

        acc = lax.fori_loop(0, rows, row_body, jnp.zeros((topk, PAGE_SIZE), F32))
        idx_ref[b] = jnp.sum(acc, axis=1, keepdims=True).astype(jnp.int32)
        return carry

    lax.fori_loop(0, n, seq_body, 0)


def _topk_sample(sc, page_table3, topk):
    n = sc.shape[0]
    rows = SC_CHUNKS * SUBLANES
    return pl.pallas_call(
        functools.partial(_topk_sample_kernel, topk=topk),
        out_shape=jax.ShapeDtypeStruct((n, topk, 1), jnp.int32),
        scratch_shapes=[pltpu.VMEM((rows, PAGE_SIZE), F32), pltpu.VMEM((rows, PAGE_SIZE), F32),
                        pltpu.VMEM((n, 1, 1, 1), F32), pltpu.VMEM((n, 1, 1, 1), jnp.int32)],
        compiler_params=pltpu.CompilerParams(vmem_limit_bytes=VMEM_LIMIT),
        name="topk_sample",
    )(sc, page_table3)


def _attn_sample_kernel(idx_ref, q_ref, knew_ref, vnew_ref, ck_ref, cv_ref, o_ref, kbuf, vbuf, sems, *, layer):
    b = pl.program_id(0)
    topk = kbuf.shape[0]

    def row_copies(j, r):
        page = lax.shift_right_logical(r, 7)
        row = r & (PAGE_SIZE - 1)
        return (pltpu.make_async_copy(ck_ref.at[layer, page, row], kbuf.at[j], sems.at[0]),
                pltpu.make_async_copy(cv_ref.at[layer, page, row], vbuf.at[j], sems.at[1]))

    def start_body(j, c):
        r = idx_ref[b, j]

        @pl.when(r >= 0)
        def _():
            ck, cv = row_copies(j, r)
            ck.start()
            cv.start()

        @pl.when(r < 0)
        def _():
            kbuf[j] = knew_ref[0]
            vbuf[j] = vnew_ref[0]
        return c

    lax.fori_loop(0, topk, start_body, 0)

    def wait_body(j, c):
        r = idx_ref[b, j]

        @pl.when(r >= 0)
        def _():
            ck, cv = row_copies(j, r)
            ck.wait()
            cv.wait()
        return c

    lax.fori_loop(0, topk, wait_body, 0)

    outs = []
    for hh in range(ATT_HEADS):
        qh = jnp.broadcast_to(q_ref[0, hh:hh + 1, :], (SUBLANES, HEAD_DIM)).astype(BF16)
        kh = kbuf[:, hh, :].astype(BF16)
        vh = vbuf[:, hh, :].astype(BF16)
        lg = _dot_nt(qh, kh)
        e = jnp.exp(lg - jnp.max(lg, axis=1, keepdims=True))
        o = _dot(e.astype(BF16), vh) / jnp.sum(e, axis=1, keepdims=True)
        outs.append(o[0:1, :])
    o_ref[0] = jnp.concatenate(outs, axis=-1).astype(BF16)


def _attn_sample(idx, q3, k_new3, v_new3, ck, cv, layer):
    n, topk = idx.shape
    head_blk = pl.BlockSpec((1, ATT_HEADS, HEAD_DIM), lambda b, ix: (b, 0, 0))
    grid_spec = pltpu.PrefetchScalarGridSpec(
        num_scalar_prefetch=1,
        grid=(n,),
        in_specs=[head_blk, head_blk, head_blk, pl.BlockSpec(memory_space=pl.ANY), pl.BlockSpec(memory_space=pl.ANY)],
        out_specs=pl.BlockSpec((1, 1, ATT_HEADS * HEAD_DIM), lambda b, ix: (b, 0, 0)),
        scratch_shapes=[pltpu.VMEM((topk, ATT_HEADS, HEAD_DIM), F32), pltpu.VMEM((topk, ATT_HEADS, HEAD_DIM), F32),
                        pltpu.SemaphoreType.DMA((2,))],
    )
    return pl.pallas_call(
        functools.partial(_attn_sample_kernel, layer=layer),
        out_shape=jax.ShapeDtypeStruct((n, 1, ATT_HEADS * HEAD_DIM), BF16),
        grid_spec=grid_spec,
        compiler_params=_cparams(("arbitrary",)),
        name="attn_sample",
    )(idx, q3, k_new3, v_new3, ck, cv)


def _sample_layer(x, mod_s, lw, tabs_s, g_final, final, layer, ck, cv, cki, page_table, pool_prev, conv_prev):
    n = x.shape[0]
    n_pages = page_table.shape[1]
    (acd, q, qi, widx, k, v, ki, v_n, pool_new_t, conv_new_t) = _pre_sample(
        x, mod_s, lw, tabs_s, jnp.swapaxes(pool_prev, 0, 1), jnp.swapaxes(conv_prev, 0, 1))
    sc = _idx_sample(page_table, qi.reshape(n, IDX_HEADS, IDX_DIM), widx.reshape(n, IDX_HEADS, 1),
                     ki.reshape(n, 1, IDX_DIM), cki, layer)
    topk = min(TOPK_MAX, (n_pages * PAGE_SIZE + 1) // 4)
    idx = _topk_sample(sc, page_table.reshape(n, n_pages, 1), topk).reshape(n, topk)
    ob = _attn_sample(idx, q.reshape(n, ATT_HEADS, HEAD_DIM), k.reshape(n, ATT_HEADS, HEAD_DIM),
                      v.reshape(n, ATT_HEADS, HEAD_DIM), ck, cv, layer)
    x_new = _post(x.reshape(1, n, D_MODEL), acd.reshape(1, n, 768), ob.reshape(1, n, 256),
                  mod_s.reshape(1, 6, n, D_MODEL), lw, g_final, final).reshape(n, D_MODEL)
    return x_new, (k.reshape(n, 1, ATT_HEADS, HEAD_DIM), v.reshape(n, 1, ATT_HEADS, HEAD_DIM),
                   ki.reshape(n, 1, IDX_DIM), v_n.reshape(n, 1, W_GRP),
                   jnp.swapaxes(pool_new_t, 0, 1), jnp.swapaxes(conv_new_t, 0, 1))


def kernel(x_prompt, x_sample, cache_k, cache_v, cache_kidx, state_pool, state_conv, page_table, c_prompt, c_sample,
           w_mod, b_mod, g_attn_norm, g_mlp_norm, w_in, w_out, g_gmlp, w_spatial, b_spatial, w_pool, s_pool, w_dw, b_dw,
           g_conv_norm, b_conv_norm, w_up, w_down, g_final):
    B, S, _ = x_prompt.shape
    n = x_sample.shape[0]
    depth = w_mod.shape[0]
    past = page_table.shape[1] * PAGE_SIZE
    tabs_p = _tables(jnp.arange(S))
    tabs_s = _tables(jnp.full((1,), past, jnp.int32))
    g_fin = g_final.reshape(1, D_MODEL)
    pad = (-B) % SUBLANES
    c_rows = jnp.concatenate([c_prompt, jnp.zeros((pad, D_MODEL), F32), c_sample], axis=0)
    xp, xs = x_prompt, x_sample.reshape(n, D_MODEL)
    outs_p, outs_s = [], []
    for l in range(depth):
        lw = _layer_weights(l, w_mod, b_mod, g_attn_norm, g_mlp_norm, w_in, w_out, g_gmlp, w_spatial, b_spatial,
                            w_pool, s_pool, w_dw, b_dw, g_conv_norm, b_conv_norm, w_up, w_down)
        mod = _modulation(c_rows, lw["w_mod"], lw["b_mod"])
        mod_p = mod[:B].reshape(B, 6, D_MODEL)
        mod_s = jnp.swapaxes(mod[B + pad:].reshape(n, 6, D_MODEL), 0, 1)
        final = l == depth - 1
        xp, op = _prompt_layer(xp, mod_p, lw, tabs_p, g_fin, final)
        xs, os_ = _sample_layer(xs, mod_s, lw, tabs_s, g_fin, final, l, cache_k, cache_v, cache_kidx,
                                page_table, state_pool[l], state_conv[l])
        outs_p.append(op)
        outs_s.append(os_)
    stack = lambda outs, j: jnp.stack([o[j] for o in outs])
    return (xp, xs.reshape(n, 1, D_MODEL),
            stack(outs_p, 0), stack(outs_p, 1), stack(outs_p, 2), stack(outs_p, 3), stack(outs_p, 4),
            stack(outs_s, 0), stack(outs_s, 1), stack(outs_s, 2), stack(outs_s, 3), stack(outs_s, 4), stack(outs_s, 5))
```

```python
import functools

import jax
import jax.numpy as jnp
import numpy as np
from jax import lax
from jax.experimental import pallas as pl
from jax.experimental.pallas import tpu as pltpu

D_MODEL = 1024
W_GRP = 256
HEAD_DIM = 64
A_HEADS = 4
ATT_HEADS = 4
IDX_HEADS = 8
IDX_DIM = 64
CHUNK = 128
TOPK_MAX = 256
PAGE_SIZE = 128
ROPE_THETA = 10000.0
POOL_WINDOWS = (2, 4, 8, 16)
POOL_STATE = 15
CONV_WIDTH = 31
CONV_STATE = 30
CONV_GROUPS = 4
D_FF = 4 * D_MODEL
EPS = 1e-6
IN_SIZES = (256, 256, 256, 256, 256, 512, 64, 8, 256, 256, 256)
D_IN = sum(IN_SIZES)
KIW_OFF = 1792
CX_OFF_RAW = 1864
D_IN_PAD = 2688
CX_OFF, DA_OFF, DG_OFF = 1920, 2176, 2432

LANES = 128
SUBLANES = 8
VMEM_LIMIT = 56 * 1024 * 1024

PRE_TM = 256
POST_TM = 512
FF_CHUNK = 1024
QB = 256
KT = 128
KT_UNROLL = 2
BISECT_ITERS = 32
POOL_HALO = 16
CONV_HALO = 32

F32 = jnp.float32
BF16 = jnp.bfloat16
NEG_INF = float("-inf")
KEY_NEG_INF = -2139095041
KEY_POS_INF = 2139095040


def _cparams(sem):
    return pltpu.CompilerParams(dimension_semantics=sem, vmem_limit_bytes=VMEM_LIMIT)


def _dot(a, b):
    return jnp.dot(a, b, preferred_element_type=F32)


def _dot_nt(a, b):
    return lax.dot_general(a, b, (((1,), (1,)), ((), ())), preferred_element_type=F32)


def _group_mean(x, gmat):
    hi = x.astype(BF16)
    lo = (x - hi.astype(F32)).astype(BF16)
    return _dot(hi, gmat) + _dot(lo, gmat)


def _rot_half(x):
    n = x.shape[-1]
    lane = lax.broadcasted_iota(jnp.int32, x.shape, x.ndim - 1)
    fwd = pltpu.roll(x, n - HEAD_DIM // 2, x.ndim - 1)
    bwd = pltpu.roll(x, HEAD_DIM // 2, x.ndim - 1)
    return jnp.where((lane % HEAD_DIM) < HEAD_DIM // 2, fwd, bwd)


def _rope(x, cos, sin_signed):
    reps = x.shape[-1] // LANES
    if reps > 1:
        cos = jnp.concatenate([cos] * reps, axis=-1)
        sin_signed = jnp.concatenate([sin_signed] * reps, axis=-1)
    return x * cos + _rot_half(x) * sin_signed


def _sigmoid(x):
    return 1.0 / (1.0 + jnp.exp(-x))


def _mod_kernel(c_ref, w_ref, b_ref, o_ref):
    c = c_ref[...]
    a = (c * _sigmoid(c)).astype(BF16)
    o_ref[...] = _dot(a, w_ref[...].astype(BF16)) + b_ref[...]


def _modulation(c_rows, w_mod_l, b_mod_l):
    n = c_rows.shape[0]
    ncol = w_mod_l.shape[1]
    tn = 1024
    return pl.pallas_call(
        _mod_kernel,
        out_shape=jax.ShapeDtypeStruct((n, ncol), F32),
        grid=(ncol // tn,),
        in_specs=[
            pl.BlockSpec((n, D_MODEL), lambda j: (0, 0)),
            pl.BlockSpec((D_MODEL, tn), lambda j: (0, j)),
            pl.BlockSpec((1, tn), lambda j: (0, j)),
        ],
        out_specs=pl.BlockSpec((n, tn), lambda j: (0, j)),
        compiler_params=_cparams(("arbitrary",)),
        name="adaln_mod",
    )(c_rows, w_mod_l, b_mod_l.reshape(1, ncol))


def _modulated_norm(x, g, shift, scale):
    ms = jnp.mean(x * x, axis=-1, keepdims=True)
    y = x * lax.rsqrt(ms + EPS) * g
    return y * (1.0 + scale) + shift


def _head_rmsnorm(av, g_row, gmat):
    ms = _group_mean(av * av, gmat)
    return av * lax.rsqrt(ms + EPS) * g_row


def _group_layernorm_silu(y, g_row, b_row, gmat):
    mu = _group_mean(y, gmat)
    d = y - mu
    var = _group_mean(d * d, gmat)
    yn = d * lax.rsqrt(var + EPS) * g_row + b_row
    return yn * _sigmoid(yn)


def _pre_prompt_kernel(x_ref, mod_ref, gattn_ref, win_ref, ggmlp_ref, wsp_ref, bsp_ref, cos_ref, sin_ref,
                       wpool_ref, spool_ref, wdw_ref, bdw_ref, gcn_ref, bcn_ref, gmat_ref,
                       acd_ref, q_ref, qi_ref, kbf_ref, vbf_ref, ki2_ref, widx_ref,
                       k_ref, v_ref, ki_ref, poolst_ref, convst_ref,
                       pool_buf, conv_buf):
    i = pl.program_id(1)
    tm = x_ref.shape[1]
    gmat = gmat_ref[...]

    @pl.when(i == 0)
    def _():
        pool_buf[0:POOL_HALO, :] = jnp.zeros((POOL_HALO, W_GRP), F32)
        conv_buf[0:CONV_HALO, :] = jnp.zeros((CONV_HALO, W_GRP), F32)

    mod = mod_ref[0]
    h = _modulated_norm(x_ref[0], gattn_ref[...], mod[0:1], mod[1:2]).astype(BF16)

    def proj(off, width):
        return _dot(h, win_ref[:, off:off + width])

    a_u = proj(0, 256)
    v_n = _head_rmsnorm(proj(256, 256), ggmlp_ref[...], gmat)
    v_bf = v_n.astype(BF16)
    lane256 = lax.broadcasted_iota(jnp.int32, (CHUNK, 256), 1)
    rr = lax.broadcasted_iota(jnp.int32, (CHUNK, CHUNK), 0)
    cc = lax.broadcasted_iota(jnp.int32, (CHUNK, CHUNK), 1)
    ws = [jnp.where(cc <= rr, wsp_ref[hh], 0.0).astype(BF16) for hh in range(A_HEADS)]
    gate_chunks = []
    for c in range(tm // CHUNK):
        vc = v_bf[c * CHUNK:(c + 1) * CHUNK, :]
        s = [_dot(ws[hh], vc) for hh in range(A_HEADS)]
        hid = lane256 // HEAD_DIM
        g = jnp.where(hid == 0, s[0], jnp.where(hid == 1, s[1], jnp.where(hid == 2, s[2], s[3])))
        gate_chunks.append(g + bsp_ref[...])
    out_a = a_u * jnp.concatenate(gate_chunks, axis=0)

    cos = cos_ref[...]
    sin = sin_ref[...]
    q = _rope(proj(512, 256), cos, sin) * (HEAD_DIM ** -0.5)
    q_ref[0] = q.astype(BF16)
    k = _rope(proj(768, 256), cos, sin)
    k_ref[0] = k
    kbf_ref[0] = k.astype(BF16)
    v = proj(1024, 256)
    v_ref[0] = v
    vbf_ref[0] = v.astype(BF16)
    qi_ref[0] = _rope(proj(1280, 512), cos, sin).astype(BF16)
    kiw = proj(KIW_OFF, LANES)
    ki_full = _rope(kiw, cos, sin)
    ki = ki_full[:, 0:IDX_DIM]
    ki_ref[0] = ki
    ki_b = ki.astype(BF16)
    ki2_ref[0] = jnp.concatenate([ki_b, ki_b], axis=-1)
    widx_ref[0] = kiw[:, IDX_DIM:IDX_DIM + IDX_HEADS] * ((IDX_HEADS * IDX_DIM) ** -0.5)

    c_x = proj(CX_OFF, 256)
    pool_buf[POOL_HALO:POOL_HALO + tm, :] = c_x
    lane128 = lax.broadcasted_iota(jnp.int32, (tm, LANES), 1)
    tpos = i * tm + lax.broadcasted_iota(jnp.int32, (tm, LANES), 0)
    upper = lane128 >= HEAD_DIM
    pooled = []
    for col, (w_lo, w_hi) in enumerate(((2, 4), (8, 16))):
        cs = slice(col * LANES, (col + 1) * LANES)
        s_lo = pool_buf[POOL_HALO:POOL_HALO + tm, cs]
        for j in range(1, w_lo):
            s_lo = s_lo + pool_buf[POOL_HALO - j:POOL_HALO - j + tm, cs]
        s_hi = pool_buf[POOL_HALO - w_lo:POOL_HALO - w_lo + tm, cs]
        for j in range(w_lo + 1, w_hi):
            s_hi = s_hi + pool_buf[POOL_HALO - j:POOL_HALO - j + tm, cs]
        win = jnp.where(upper, w_hi, w_lo)
        cnt = jnp.minimum(win, tpos + 1).astype(F32)
        pooled.append(jnp.where(upper, s_lo + s_hi, s_lo) / cnt)
    diff = (jnp.concatenate(pooled, axis=-1) - c_x).astype(BF16)
    out_c = _dot(diff, wpool_ref[...]) * spool_ref[...]
    poolst_ref[0] = pool_buf[tm:tm + POOL_HALO, :]
    pool_buf[0:POOL_HALO, :] = pool_buf[tm:tm + POOL_HALO, :]

    glu = proj(DA_OFF, 256) * _sigmoid(proj(DG_OFF, 256))
    conv_buf[CONV_HALO:CONV_HALO + tm, :] = glu
    base = CONV_HALO - CONV_STATE
    y = conv_buf[base:base + tm, :] * wdw_ref[0:1, :]
    for j in range(1, CONV_WIDTH):
        y = y + conv_buf[base + j:base + j + tm, :] * wdw_ref[j:j + 1, :]
    y = y + bdw_ref[...]
    out_d = _group_layernorm_silu(y, gcn_ref[...], bcn_ref[...], gmat)
    convst_ref[0] = conv_buf[tm:tm + CONV_HALO, :]
    conv_buf[0:CONV_HALO, :] = conv_buf[tm:tm + CONV_HALO, :]

    acd_ref[0] = jnp.concatenate([out_a, out_c, out_d], axis=-1).astype(BF16)


def _pre_prompt(x, mod, lw, tabs):
    B, S, _ = x.shape
    tm = min(PRE_TM, S)
    n_t = S // tm
    tok = lambda w: pl.BlockSpec((1, tm, w), lambda b, i: (b, i, 0))
    const = lambda shape: pl.BlockSpec(shape, lambda b, i: tuple(0 for _ in shape))
    in_specs = [
        tok(D_MODEL),
        pl.BlockSpec((1, 6, D_MODEL), lambda b, i: (b, 0, 0)),
        const((1, D_MODEL)),
        const((D_MODEL, D_IN_PAD)),
        const((1, W_GRP)),
        const((A_HEADS, CHUNK, CHUNK)),
        const((CHUNK, W_GRP)),
        pl.BlockSpec((tm, LANES), lambda b, i: (i, 0)),
        pl.BlockSpec((tm, LANES), lambda b, i: (i, 0)),
        const((W_GRP, W_GRP)),
        const((1, W_GRP)),
        const((32, W_GRP)),
        const((1, W_GRP)),
        const((1, W_GRP)),
        const((1, W_GRP)),
        const((W_GRP, W_GRP)),
    ]
    out_shapes = [
        jax.ShapeDtypeStruct((B, S, 768), BF16),
        jax.ShapeDtypeStruct((B, S, 256), BF16),
        jax.ShapeDtypeStruct((B, S, 512), BF16),
        jax.ShapeDtypeStruct((B, S, 256), BF16),
        jax.ShapeDtypeStruct((B, S, 256), BF16),
        jax.ShapeDtypeStruct((B, S, LANES), BF16),
        jax.ShapeDtypeStruct((B, S, IDX_HEADS), F32),
        jax.ShapeDtypeStruct((B, S, 256), F32),
        jax.ShapeDtypeStruct((B, S, 256), F32),
        jax.ShapeDtypeStruct((B, S, IDX_DIM), F32),
        jax.ShapeDtypeStruct((B, POOL_HALO, W_GRP), F32),
        jax.ShapeDtypeStruct((B, CONV_HALO, W_GRP), F32),
    ]
    out_specs = [tok(768), tok(256), tok(512), tok(256), tok(256), tok(LANES), tok(IDX_HEADS),
                 tok(256), tok(256), tok(IDX_DIM),
                 pl.BlockSpec((1, POOL_HALO, W_GRP), lambda b, i: (b, 0, 0)),
                 pl.BlockSpec((1, CONV_HALO, W_GRP), lambda b, i: (b, 0, 0))]
    return pl.pallas_call(
        _pre_prompt_kernel,
        out_shape=out_shapes,
        grid=(B, n_t),
        in_specs=in_specs,
        out_specs=out_specs,
        scratch_shapes=[pltpu.VMEM((POOL_HALO + tm, W_GRP), F32), pltpu.VMEM((CONV_HALO + tm, W_GRP), F32)],
        compiler_params=_cparams(("arbitrary", "arbitrary")),
        name="pre_prompt",
    )(x, mod, lw["g_attn"], lw["w_in"], lw["g_gmlp"], lw["w_spatial"], lw["b_spatial_t"], tabs["cos"], tabs["sin"],
      lw["w_pool_bd"], lw["s_pool"], lw["w_dw"], lw["b_dw"], lw["g_cn"], lw["b_cn"], tabs["gmat"])


def _key_to_float(key):
    bits = jnp.where(key >= 0, key, key ^ jnp.int32(0x7FFFFFFF))
    return pltpu.bitcast(bits, F32)


def _dsa_prompt_kernel(q_ref, qi_ref, w_ref, ki2_ref, k_ref, v_ref, o_ref, sc_ref, lg_ref, wb_ref, *, topk):
    i = pl.program_id(1)
    qb = q_ref.shape[1]
    n_rg = qb // KT
    tiles_per_blk = qb // KT
    n_it = ((i + 1) * tiles_per_blk + KT_UNROLL - 1) // KT_UNROLL
    lane = lax.broadcasted_iota(jnp.int32, (KT, KT), 1)
    row = lax.broadcasted_iota(jnp.int32, (KT, KT), 0)
    low_half = lane < HEAD_DIM

    for g in range(n_rg):
        w = w_ref[0, g * KT:(g + 1) * KT, :]
        for hh in range(IDX_HEADS):
            wb_ref[g, hh] = jnp.broadcast_to(w[:, hh:hh + 1], (KT, KT))

    lhs = []
    for g in range(n_rg):
        per_head = []
        for p in range(IDX_HEADS // 2):
            slab = qi_ref[0, g * KT:(g + 1) * KT, p * LANES:(p + 1) * LANES]
            zero = jnp.zeros_like(slab)
            per_head.append(jnp.where(low_half, slab, zero))
            per_head.append(jnp.where(low_half, zero, slab))
        lhs.append(per_head)

    def score_body(it, carry):
        for u in range(KT_UNROLL):
            kt = it * KT_UNROLL + u
            kc = ki2_ref[0, pl.ds(pl.multiple_of(kt * KT, KT), KT), :]
            for g in range(n_rg):
                acc = jnp.zeros((KT, KT), F32)
                for hh in range(IDX_HEADS):
                    acc = acc + wb_ref[g, hh] * jnp.maximum(_dot_nt(lhs[g][hh], kc), 0.0)
                q_pos = (i * tiles_per_blk + g) * KT + row
                sc_ref[g, kt] = jnp.where(kt * KT + lane <= q_pos, acc, NEG_INF)
        return carry

    lax.fori_loop(0, n_it, score_body, 0)

    def count_ge(thr):
        def body(it, accs):
            accs = list(accs)
            for u in range(KT_UNROLL):
                kt = it * KT_UNROLL + u
                for g in range(n_rg):
                    accs[g] = accs[g] + jnp.where(sc_ref[g, kt] >= thr[g], 1.0, 0.0)
            return tuple(accs)
        accs = lax.fori_loop(0, n_it, body, tuple(jnp.zeros((KT, KT), F32) for _ in range(n_rg)))
        return [jnp.sum(a, axis=1, keepdims=True) for a in accs]

    colrow = lax.broadcasted_iota(jnp.int32, (KT, 1), 0)
    k_row = [jnp.minimum(topk, (i * tiles_per_blk + g) * KT + colrow + 1).astype(F32) for g in range(n_rg)]

    def bisect_body(_, carry):
        los, his = carry
        mids = [(lo >> 1) + (hi >> 1) + (lo & hi & 1) for lo, hi in zip(los, his)]
        cnts = count_ge([_key_to_float(m) for m in mids])
        ge = [c >= kr for c, kr in zip(cnts, k_row)]
        return (tuple(jnp.where(c, m, lo) for c, m, lo in zip(ge, mids, los)),
                tuple(jnp.where(c, hi, m) for c, m, hi in zip(ge, mids, his)))

    lo0 = tuple(jnp.full((KT, 1), KEY_NEG_INF + 1, jnp.int32) for _ in range(n_rg))
    hi0 = tuple(jnp.full((KT, 1), KEY_POS_INF + 1, jnp.int32) for _ in range(n_rg))
    los, _ = lax.fori_loop(0, BISECT_ITERS, bisect_body, (lo0, hi0))
    thr = [_key_to_float(lo) for lo in los]

    def count_sel(cut):
        def body(it, accs):
            accs = list(accs)
            for u in range(KT_UNROLL):
                kt = it * KT_UNROLL + u
                for g in range(n_rg):
                    s = sc_ref[g, kt]
                    sel = (s > thr[g]) | ((s == thr[g]) & (kt * KT + lane <= cut[g]))
                    accs[g] = accs[g] + jnp.where(sel, 1.0, 0.0)
            return tuple(accs)
        accs = lax.fori_loop(0, n_it, body, tuple(jnp.zeros((KT, KT), F32) for _ in range(n_rg)))
        return [jnp.sum(a, axis=1, keepdims=True) for a in accs]

    n_keys = n_it * KT_UNROLL * KT
    n_ge = count_ge(thr)
    excess = n_ge[0] > k_row[0]
    for g in range(1, n_rg):
        excess = excess | (n_ge[g] > k_row[g])
    any_excess = jnp.max(jnp.where(excess, 1.0, 0.0)) > 0.0

    def tie_cut(_):
        def body(_, carry):
            los_c, his_c = carry
            mids = [(lo + hi) >> 1 for lo, hi in zip(los_c, his_c)]
            cnts = count_sel(mids)
            ge = [c >= kr for c, kr in zip(cnts, k_row)]
            return (tuple(jnp.where(c, lo, m) for c, m, lo in zip(ge, mids, los_c)),
                    tuple(jnp.where(c, m, hi) for c, m, hi in zip(ge, mids, his_c)))
        lo_c = tuple(jnp.full((KT, 1), -1, jnp.int32) for _ in range(n_rg))
        hi_c = tuple(jnp.full((KT, 1), 1, jnp.int32) * (n_keys - 1) for _ in range(n_rg))
        _, his_c = lax.fori_loop(0, 13, body, (lo_c, hi_c))
        return his_c

    def no_tie_cut(_):
        return tuple(jnp.full((KT, 1), 1, jnp.int32) * n_keys for _ in range(n_rg))

    cut = lax.cond(any_excess, tie_cut, no_tie_cut, 0)

    def bias_body(it, carry):
        for u in range(KT_UNROLL):
            kt = it * KT_UNROLL + u
            for g in range(n_rg):
                s = sc_ref[g, kt]
                sel = (s > thr[g]) | ((s == thr[g]) & (kt * KT + lane <= cut[g]))
                sc_ref[g, kt] = jnp.where(sel, 0.0, NEG_INF)
        return carry

    lax.fori_loop(0, n_it, bias_body, 0)

    for p in range(ATT_HEADS // 2):
        cs = slice(p * LANES, (p + 1) * LANES)
        outs = [[None, None] for _ in range(n_rg)]
        for half in range(2):
            qh = []
            for g in range(n_rg):
                slab = q_ref[0, g * KT:(g + 1) * KT, cs]
                zero = jnp.zeros_like(slab)
                qh.append(jnp.where(low_half, slab, zero) if half == 0 else jnp.where(low_half, zero, slab))

            def logit_body(it, mruns):
                mruns = list(mruns)
                for u in range(KT_UNROLL):
                    kt = it * KT_UNROLL + u
                    kc = k_ref[0, pl.ds(pl.multiple_of(kt * KT, KT), KT), cs]
                    for g in range(n_rg):
                        lgt = _dot_nt(qh[g], kc) + sc_ref[g, kt]
                        lg_ref[g, kt] = lgt
                        mruns[g] = jnp.maximum(mruns[g], lgt)
                return tuple(mruns)

            mruns = lax.fori_loop(0, n_it, logit_body, tuple(jnp.full((KT, KT), NEG_INF, F32) for _ in range(n_rg)))
            mx = [jnp.max(m, axis=1, keepdims=True) for m in mruns]

            def pv_body(it, carry):
                lruns, accs = carry
                lruns, accs = list(lruns), list(accs)
                for u in range(KT_UNROLL):
                    kt = it * KT_UNROLL + u
                    vc = v_ref[0, pl.ds(pl.multiple_of(kt * KT, KT), KT), cs]
                    for g in range(n_rg):
                        pe = jnp.exp(lg_ref[g, kt] - mx[g])
                        lruns[g] = lruns[g] + pe
                        accs[g] = accs[g] + _dot(pe.astype(BF16), vc)
                return tuple(lruns), tuple(accs)

            zeros = tuple(jnp.zeros((KT, KT), F32) for _ in range(n_rg))
            lruns, accs = lax.fori_loop(0, n_it, pv_body, (zeros, zeros))
            for g in range(n_rg):
                outs[g][half] = accs[g] / jnp.sum(lruns[g], axis=1, keepdims=True)
        for g in range(n_rg):
            o_ref[0, g * KT:(g + 1) * KT, cs] = jnp.where(low_half, outs[g][0], outs[g][1]).astype(BF16)


def _dsa_prompt(q, qi, widx, ki2, kbf, vbf):
    B, S, _ = q.shape
    qb = min(QB, S)
    n_q = S // qb
    n_rg = qb // KT
    n_kt = S // KT + KT_UNROLL
    blk = lambda w: pl.BlockSpec((1, qb, w), lambda b, i: (b, i, 0))
    full = lambda w: pl.BlockSpec((1, S, w), lambda b, i: (b, 0, 0))
    return pl.pallas_call(
        functools.partial(_dsa_prompt_kernel, topk=min(TOPK_MAX, S // 4)),
        out_shape=jax.ShapeDtypeStruct((B, S, 256), BF16),
        grid=(B, n_q),
        in_specs=[blk(256), blk(512), blk(IDX_HEADS), full(LANES), full(256), full(256)],
        out_specs=blk(256),
        scratch_shapes=[pltpu.VMEM((n_rg, n_kt, KT, KT), F32), pltpu.VMEM((n_rg, n_kt, KT, KT), F32),
                        pltpu.VMEM((n_rg, IDX_HEADS, KT, KT), F32)],
        compiler_params=_cparams(("arbitrary", "arbitrary")),
        name="dsa_prompt",
    )(q, qi, widx, ki2, kbf, vbf)


def _post_kernel(x_ref, acd_ref, ob_ref, mod_ref, gmlp_ref, woacd_ref, wob_ref, wup_ref, wdn_ref, gfin_ref, o_ref,
                 *, final):
    mod = mod_ref[0]
    mixed = _dot(acd_ref[0], woacd_ref[...]) + _dot(ob_ref[0], wob_ref[...])
    x1 = x_ref[0] + mod[2] * mixed
    h2 = _modulated_norm(x1, gmlp_ref[...], mod[3], mod[4]).astype(BF16)
    acc = jnp.zeros_like(x1)
    for c in range(D_FF // FF_CHUNK):
        u = jnp.maximum(_dot(h2, wup_ref[:, c * FF_CHUNK:(c + 1) * FF_CHUNK]), 0.0)
        acc = acc + _dot((u * u).astype(BF16), wdn_ref[c * FF_CHUNK:(c + 1) * FF_CHUNK, :])
    x2 = x1 + mod[5] * acc
    if final:
        ms = jnp.mean(x2 * x2, axis=-1, keepdims=True)
        x2 = x2 * lax.rsqrt(ms + EPS) * gfin_ref[...]
    o_ref[0] = x2


def _post(x, acd, ob, mod, lw, g_final, final):
    G, T, _ = x.shape
    tm = min(POST_TM, T)
    R = mod.shape[2]
    tok = lambda w: pl.BlockSpec((1, tm, w), lambda b, i: (b, i, 0))
    const = lambda shape: pl.BlockSpec(shape, lambda b, i: tuple(0 for _ in shape), pipeline_mode=pl.Buffered(1))
    if R == 1:
        mod_spec = pl.BlockSpec((1, 6, 1, D_MODEL), lambda b, i: (b, 0, 0, 0))
    else:
        mod_spec = pl.BlockSpec((1, 6, tm, D_MODEL), lambda b, i: (b, 0, i, 0))
    return pl.pallas_call(
        functools.partial(_post_kernel, final=final),
        out_shape=jax.ShapeDtypeStruct((G, T, D_MODEL), F32),
        grid=(G, T // tm),
        in_specs=[tok(D_MODEL), tok(768), tok(256), mod_spec, const((1, D_MODEL)),
                  const((768, D_MODEL)), const((256, D_MODEL)), const((D_MODEL, D_FF)), const((D_FF, D_MODEL)),
                  const((1, D_MODEL))],
        out_specs=tok(D_MODEL),
        compiler_params=_cparams(("arbitrary", "arbitrary")),
        name="post_final" if final else "post",
    )(x, acd, ob, mod, lw["g_mlp"], lw["w_out_acd"], lw["w_out_b"], lw["w_up"], lw["w_down"], g_final)


def _layer_weights(l, w_mod, b_mod, g_attn_norm, g_mlp_norm, w_in, w_out, g_gmlp, w_spatial, b_spatial,
                   w_pool, s_pool, w_dw, b_dw, g_conv_norm, b_conv_norm, w_up, w_down):
    w_in_l = w_in[l]
    w_in_pad = jnp.concatenate(
        [w_in_l[:, :CX_OFF_RAW], jnp.zeros((D_MODEL, CX_OFF - CX_OFF_RAW), w_in_l.dtype), w_in_l[:, CX_OFF_RAW:]],
        axis=1).astype(BF16)
    w_pool_bd = jnp.zeros((W_GRP, W_GRP), F32)
    for g in range(len(POOL_WINDOWS)):
        w_pool_bd = w_pool_bd.at[g * 64:(g + 1) * 64, g * 64:(g + 1) * 64].set(w_pool[l, g])
    w_out_l = w_out[l]
    return dict(
        w_mod=w_mod[l], b_mod=b_mod[l],
        g_attn=g_attn_norm[l].reshape(1, D_MODEL), g_mlp=g_mlp_norm[l].reshape(1, D_MODEL),
        w_in=w_in_pad,
        g_gmlp=g_gmlp[l].reshape(1, W_GRP),
        w_spatial=w_spatial[l],
        b_spatial_t=jnp.repeat(b_spatial[l].T, HEAD_DIM, axis=1),
        w_sp0=jnp.repeat(w_spatial[l, :, 0, 0], HEAD_DIM).reshape(1, W_GRP),
        b_sp0=jnp.repeat(b_spatial[l, :, 0], HEAD_DIM).reshape(1, W_GRP),
        w_pool_bd=w_pool_bd.astype(BF16), s_pool=s_pool[l].reshape(1, W_GRP),
        w_dw=jnp.concatenate([w_dw[l], jnp.zeros((1, W_GRP), F32)], axis=0), b_dw=b_dw[l].reshape(1, W_GRP),
        g_cn=g_conv_norm[l].reshape(1, W_GRP), b_cn=b_conv_norm[l].reshape(1, W_GRP),
        w_out_acd=jnp.concatenate([w_out_l[0:256], w_out_l[512:1024]], axis=0).astype(BF16),
        w_out_b=w_out_l[256:512].astype(BF16),
        w_up=w_up[l].astype(BF16), w_down=w_down[l].astype(BF16),
    )


def _tables(pos):
    half = HEAD_DIM // 2
    freqs = ROPE_THETA ** (-jnp.arange(half, dtype=F32) / half)
    ang = pos.astype(F32)[:, None] * freqs[None, :]
    cos = jnp.tile(jnp.cos(ang), (1, LANES // half))
    sin = jnp.sin(ang)
    sin_signed = jnp.tile(jnp.concatenate([-sin, sin], axis=1), (1, LANES // HEAD_DIM))
    gid = jnp.arange(W_GRP) // HEAD_DIM
    gmat = jnp.where(gid[:, None] == gid[None, :], 1.0 / HEAD_DIM, 0.0).astype(BF16)
    return dict(cos=cos, sin=sin_signed, gmat=gmat)


def _prompt_layer(x, mod_p, lw, tabs, g_final, final):
    B, S, _ = x.shape
    (acd, q, qi, kbf, vbf, ki2, widx, k, v, ki, poolst, convst) = _pre_prompt(x, mod_p, lw, tabs)
    ob = _dsa_prompt(q, qi, widx, ki2, kbf, vbf)
    x_new = _post(x, acd, ob, mod_p.reshape(B, 6, 1, D_MODEL), lw, g_final, final)
    return x_new, (k.reshape(B, S, ATT_HEADS, HEAD_DIM), v.reshape(B, S, ATT_HEADS, HEAD_DIM), ki,
                   poolst[:, POOL_HALO - POOL_STATE:], convst[:, CONV_HALO - CONV_STATE:])


def _pre_sample_kernel(x_ref, mod_ref, gattn_ref, win_ref, ggmlp_ref, wsp0_ref, bsp0_ref, cos_ref, sin_ref,
                       wpool_ref, spool_ref, wdw_ref, bdw_ref, gcn_ref, bcn_ref, gmat_ref, poolprev_ref, convprev_ref,
                       acd_ref, q_ref, qi_ref, widx_ref, k_ref, v_ref, ki_ref, vn_ref, poolnew_ref, convnew_ref):
    gmat = gmat_ref[...]
    h = _modulated_norm(x_ref[...], gattn_ref[...], mod_ref[0], mod_ref[1]).astype(BF16)

    def proj(off, width):
        return _dot(h, win_ref[:, off:off + width])

    a_u = proj(0, 256)
    v_n = _head_rmsnorm(proj(256, 256), ggmlp_ref[...], gmat)
    vn_ref[...] = v_n
    out_a = a_u * (wsp0_ref[...] * v_n + bsp0_ref[...])

    cos = cos_ref[...]
    sin = sin_ref[...]
    q_ref[...] = _rope(proj(512, 256), cos, sin) * (HEAD_DIM ** -0.5)
    k_ref[...] = _rope(proj(768, 256), cos, sin)
    v_ref[...] = proj(1024, 256)
    qi_ref[...] = _rope(proj(1280, 512), cos, sin).astype(BF16)
    kiw = proj(KIW_OFF, LANES)
    ki_ref[...] = _rope(kiw, cos, sin)[:, 0:IDX_DIM]
    widx_ref[...] = kiw[:, IDX_DIM:IDX_DIM + IDX_HEADS] * ((IDX_HEADS * IDX_DIM) ** -0.5)

    c_x = proj(CX_OFF, 256)
    sums = {}
    run = c_x
    for j in range(1, max(POOL_WINDOWS)):
        run = run + poolprev_ref[POOL_STATE - j]
        if j + 1 in POOL_WINDOWS:
            sums[j + 1] = run * (1.0 / (j + 1))
    gid = lax.broadcasted_iota(jnp.int32, c_x.shape, 1) // HEAD_DIM
    pooled = jnp.where(gid == 0, sums[2], jnp.where(gid == 1, sums[4], jnp.where(gid == 2, sums[8], sums[16])))
    diff = (pooled - c_x).astype(BF16)
    out_c = _dot(diff, wpool_ref[...]) * spool_ref[...]
    for r in range(POOL_STATE - 1):
        poolnew_ref[r] = poolprev_ref[r + 1]
    poolnew_ref[POOL_STATE - 1] = c_x

    glu = proj(DA_OFF, 256) * _sigmoid(proj(DG_OFF, 256))
    y = glu * wdw_ref[CONV_STATE:CONV_STATE + 1, :]
    for j in range(CONV_STATE):
        y = y + convprev_ref[j] * wdw_ref[j:j + 1, :]
    y = y + bdw_ref[...]
    out_d = _group_layernorm_silu(y, gcn_ref[...], bcn_ref[...], gmat)
    for r in range(CONV_STATE - 1):
        convnew_ref[r] = convprev_ref[r + 1]
    convnew_ref[CONV_STATE - 1] = glu

    acd_ref[...] = jnp.concatenate([out_a, out_c, out_d], axis=-1).astype(BF16)


def _pre_sample(x, mod_s, lw, tabs_s, pool_prev_t, conv_prev_t):
    n = x.shape[0]
    out_shapes = [
        jax.ShapeDtypeStruct((n, 768), BF16),
        jax.ShapeDtypeStruct((n, 256), F32),
        jax.ShapeDtypeStruct((n, 512), BF16),
        jax.ShapeDtypeStruct((n, IDX_HEADS), F32),
        jax.ShapeDtypeStruct((n, 256), F32),
        jax.ShapeDtypeStruct((n, 256), F32),
        jax.ShapeDtypeStruct((n, IDX_DIM), F32),
        jax.ShapeDtypeStruct((n, 256), F32),
        jax.ShapeDtypeStruct((POOL_STATE, n, W_GRP), F32),
        jax.ShapeDtypeStruct((CONV_STATE, n, W_GRP), F32),
    ]
    return pl.pallas_call(
        _pre_sample_kernel,
        out_shape=out_shapes,
        compiler_params=pltpu.CompilerParams(vmem_limit_bytes=VMEM_LIMIT),
        name="pre_sample",
    )(x, mod_s, lw["g_attn"], lw["w_in"], lw["g_gmlp"], lw["w_sp0"], lw["b_sp0"], tabs_s["cos"], tabs_s["sin"],
      lw["w_pool_bd"], lw["s_pool"], lw["w_dw"], lw["b_dw"], lw["g_cn"], lw["b_cn"], tabs_s["gmat"],
      pool_prev_t, conv_prev_t)


SC_CHUNKS = 17
PAGES_PER_CHUNK = 8
NEW_CHUNK = 16
ATT_PAGES = 16


def _idx_sample_kernel(pt_ref, qi_ref, w_ref, kinew_ref, cki_ref, sc_ref, kbuf, sems, *, layer):
    b = pl.program_id(0)
    nb = pl.num_programs(0)
    n_pages = kbuf.shape[1]

    def page_copy(bb, p, slot):
        return pltpu.make_async_copy(cki_ref.at[layer, pt_ref[bb, p]], kbuf.at[slot, p], sems.at[slot])

    def start_all(bb, slot):
        def body(p, c):
            page_copy(bb, p, slot).start()
            return c
        lax.fori_loop(0, n_pages, body, 0)

    slot = b % 2

    @pl.when(b == 0)
    def _():
        start_all(0, 0)

    @pl.when(b + 1 < nb)
    def _():
        start_all(b + 1, 1 - slot)

    def wait_body(p, c):
        page_copy(b, p, slot).wait()
        return c
    lax.fori_loop(0, n_pages, wait_body, 0)

    qi = qi_ref[0]
    w = w_ref[0]

    def chunk_body(c, carry):
        tiles = [kbuf[slot, c * PAGES_PER_CHUNK + j].astype(BF16) for j in range(PAGES_PER_CHUNK)]
        d = jnp.maximum(_dot(qi, jnp.concatenate(tiles, axis=1)), 0.0) * w
        s = jnp.sum(d, axis=0, keepdims=True)
        for j in range(PAGES_PER_CHUNK):
            sc_ref[0, c, j:j + 1, :] = s[:, j * PAGE_SIZE:(j + 1) * PAGE_SIZE]
        return carry

    lax.fori_loop(0, n_pages // PAGES_PER_CHUNK, chunk_body, 0)

    k_new = kinew_ref[0].astype(BF16).astype(F32)
    d_new = jnp.sum(qi.astype(F32) * k_new, axis=1, keepdims=True)
    s_new = jnp.sum(jnp.maximum(d_new, 0.0) * w, axis=0, keepdims=True)
    lane = lax.broadcasted_iota(jnp.int32, (1, PAGE_SIZE), 1)
    sc_ref[0, NEW_CHUNK] = jnp.full((SUBLANES, PAGE_SIZE), NEG_INF, F32)
    sc_ref[0, NEW_CHUNK, 0:1, :] = jnp.where(lane == 0, s_new, NEG_INF)


def _idx_sample(page_table, qi3, w3, ki_new3, cki, layer):
    n, n_pages = page_table.shape
    assert n_pages == (NEW_CHUNK * PAGES_PER_CHUNK)
    grid_spec = pltpu.PrefetchScalarGridSpec(
        num_scalar_prefetch=1,
        grid=(n,),
        in_specs=[
            pl.BlockSpec((1, IDX_HEADS, IDX_DIM), lambda b, pt: (b, 0, 0)),
            pl.BlockSpec((1, IDX_HEADS, 1), lambda b, pt: (b, 0, 0)),
            pl.BlockSpec((1, 1, IDX_DIM), lambda b, pt: (b, 0, 0)),
            pl.BlockSpec(memory_space=pl.ANY),
        ],
        out_specs=pl.BlockSpec((1, SC_CHUNKS, SUBLANES, PAGE_SIZE), lambda b, pt: (b, 0, 0, 0)),
        scratch_shapes=[pltpu.VMEM((2, n_pages, IDX_DIM, PAGE_SIZE), F32), pltpu.SemaphoreType.DMA((2,))],
    )
    return pl.pallas_call(
        functools.partial(_idx_sample_kernel, layer=layer),
        out_shape=jax.ShapeDtypeStruct((n, SC_CHUNKS, SUBLANES, PAGE_SIZE), F32),
        grid_spec=grid_spec,
        compiler_params=_cparams(("arbitrary",)),
        name="idx_sample",
    )(page_table, qi3, w3, ki_new3, cki)


def _topk_sample_kernel(sc_ref, thr_ref, cut_ref, *, topk):
    n = sc_ref.shape[0]
    shape4 = sc_ref.shape

    def total(x):
        x = jnp.sum(x, axis=1, keepdims=True)
        x = jnp.sum(x, axis=2, keepdims=True)
        return jnp.sum(x, axis=3, keepdims=True)

    kf = float(topk)

    def bisect_body(_, carry):
        lo, hi = carry
        mid = (lo >> 1) + (hi >> 1) + (lo & hi & 1)
        cnt = total(jnp.where(sc_ref[...] >= _key_to_float(mid), 1.0, 0.0))
        ge = cnt >= kf
        return jnp.where(ge, mid, lo), jnp.where(ge, hi, mid)

    lo0 = jnp.full((n, 1, 1, 1), KEY_NEG_INF + 1, jnp.int32)
    hi0 = jnp.full((n, 1, 1, 1), KEY_POS_INF + 1, jnp.int32)
    lo, _ = lax.fori_loop(0, BISECT_ITERS, bisect_body, (lo0, hi0))
    thr = _key_to_float(lo)

    kidx = (lax.broadcasted_iota(jnp.int32, shape4, 1) * (SUBLANES * PAGE_SIZE)
            + lax.broadcasted_iota(jnp.int32, shape4, 2) * PAGE_SIZE
            + lax.broadcasted_iota(jnp.int32, shape4, 3))

    def count_sel(cut):
        s = sc_ref[...]
        sel = (s > thr) | ((s == thr) & (kidx <= cut))
        return total(jnp.where(sel, 1.0, 0.0))

    n_keys = SC_CHUNKS * SUBLANES * PAGE_SIZE

    def cut_body(_, carry):
        lo_c, hi_c = carry
        mid = (lo_c + hi_c) >> 1
        ge = count_sel(mid) >= kf
        return jnp.where(ge, lo_c, mid), jnp.where(ge, mid, hi_c)

    lo_c = jnp.full((n, 1, 1, 1), -1, jnp.int32)
    hi_c = jnp.full((n, 1, 1, 1), n_keys - 1, jnp.int32)
    _, cut = lax.fori_loop(0, 15, cut_body, (lo_c, hi_c))
    thr_ref[...] = thr
    cut_ref[...] = cut


def _topk_sample(sc, topk):
    n = sc.shape[0]
    return pl.pallas_call(
        functools.partial(_topk_sample_kernel, topk=topk),
        out_shape=[jax.ShapeDtypeStruct((n, 1, 1, 1), F32), jax.ShapeDtypeStruct((n, 1, 1, 1), jnp.int32)],
        compiler_params=pltpu.CompilerParams(vmem_limit_bytes=VMEM_LIMIT),
        name="topk_sample",
    )(sc)


def _attn_sample_kernel(pt_ref, q_ref, knew_ref, vnew_ref, sc_ref, thr_ref, cut_ref, ck_ref, cv_ref, o_ref,
                        buf, lg_ref, qb_ref, sems, *, layer):
    b = pl.program_id(0)
    nb = pl.num_programs(0)
    n_pages = lg_ref.shape[1]
    n_ch = n_pages // ATT_PAGES

    def chunk_copy(src_ref, bb, u, j, slot):
        return pltpu.make_async_copy(src_ref.at[layer, pt_ref[bb, u * ATT_PAGES + j]], buf.at[slot, j], sems.at[slot])

    def start_chunk(src_ref, bb, u, slot):
        for j in range(ATT_PAGES):
            chunk_copy(src_ref, bb, u, j, slot).start()

    def wait_chunk(src_ref, bb, u, slot):
        for j in range(ATT_PAGES):
            chunk_copy(src_ref, bb, u, j, slot).wait()

    @pl.when(b == 0)
    def _():
        start_chunk(ck_ref, 0, 0, 0)

    for hh in range(ATT_HEADS):
        qb_ref[hh] = jnp.broadcast_to(q_ref[0, hh], (HEAD_DIM, PAGE_SIZE))

    def k_body(u, carry):
        slot = u % 2

        @pl.when(u + 1 < n_ch)
        def _():
            start_chunk(ck_ref, b, u + 1, 1 - slot)

        @pl.when(u + 1 == n_ch)
        def _():
            start_chunk(cv_ref, b, 0, 1 - slot)

        wait_chunk(ck_ref, b, u, slot)
        for j in range(ATT_PAGES):
            for hh in range(ATT_HEADS):
                lg = jnp.sum(buf[slot, j, hh] * qb_ref[hh], axis=0, keepdims=True)
                lg_ref[hh, pl.ds(u * ATT_PAGES + j, 1), :] = lg
        return carry

    lax.fori_loop(0, n_ch, k_body, 0)

    rows = SC_CHUNKS * SUBLANES
    s2 = sc_ref[0].reshape(rows, PAGE_SIZE)
    thr = thr_ref[0, 0]
    cut = cut_ref[0, 0]
    kidx = (lax.broadcasted_iota(jnp.int32, (rows, PAGE_SIZE), 0) * PAGE_SIZE
            + lax.broadcasted_iota(jnp.int32, (rows, PAGE_SIZE), 1))
    sel = jnp.where((s2 > thr) | ((s2 == thr) & (kidx <= cut)), 1.0, 0.0)
    sel_past = sel[0:n_pages, :] > 0.0
    sel_new = sel[n_pages:n_pages + 1, 0:1] > 0.0
    e_new, den = [], []
    for hh in range(ATT_HEADS):
        lgm = jnp.where(sel_past, lg_ref[hh], NEG_INF)
        l_new = jnp.sum(knew_ref[0, hh] * q_ref[0, hh], axis=0, keepdims=True)
        l_new = jnp.where(sel_new, l_new, NEG_INF)
        m = jnp.maximum(jnp.max(jnp.max(lgm, axis=1, keepdims=True), axis=0, keepdims=True), l_new)
        e = jnp.exp(lgm - m)
        lg_ref[hh] = e
        e_new.append(jnp.exp(l_new - m))
        den.append(jnp.sum(jnp.sum(e, axis=1, keepdims=True), axis=0, keepdims=True) + e_new[hh])

    def v_body(u, accs):
        slot = u % 2

        @pl.when(u + 1 < n_ch)
        def _():
            start_chunk(cv_ref, b, u + 1, 1 - slot)

        @pl.when((u + 1 == n_ch) & (b + 1 < nb))
        def _():
            start_chunk(ck_ref, b + 1, 0, 1 - slot)

        wait_chunk(cv_ref, b, u, slot)
        accs = list(accs)
        for j in range(ATT_PAGES):
            for hh in range(ATT_HEADS):
                p_row = lg_ref[hh, pl.ds(u * ATT_PAGES + j, 1), :]
                accs[hh] = accs[hh] + buf[slot, j, hh] * p_row
        return tuple(accs)

    accs = lax.fori_loop(0, n_ch, v_body, tuple(jnp.zeros((HEAD_DIM, PAGE_SIZE), F32) for _ in range(ATT_HEADS)))
    for hh in range(ATT_HEADS):
        o = jnp.sum(accs[hh], axis=1, keepdims=True) + e_new[hh] * vnew_ref[0, hh]
        o_ref[0, hh] = o / den[hh]


def _attn_sample(page_table, q4, k_new4, v_new4, sc, thr, cut, ck_t, cv_t, layer):
    n, n_pages = page_table.shape
    col_blk = pl.BlockSpec((1, ATT_HEADS, HEAD_DIM, 1), lambda b, pt: (b, 0, 0, 0))
    one_blk = pl.BlockSpec((1, 1, 1, 1), lambda b, pt: (b, 0, 0, 0))
    grid_spec = pltpu.PrefetchScalarGridSpec(
        num_scalar_prefetch=1,
        grid=(n,),
        in_specs=[col_blk, col_blk, col_blk,
                  pl.BlockSpec((1, SC_CHUNKS, SUBLANES, PAGE_SIZE), lambda b, pt: (b, 0, 0, 0)),
                  one_blk, one_blk, pl.BlockSpec(memory_space=pl.ANY), pl.BlockSpec(memory_space=pl.ANY)],
        out_specs=col_blk,
        scratch_shapes=[pltpu.VMEM((2, ATT_PAGES, ATT_HEADS, HEAD_DIM, PAGE_SIZE), F32),
                        pltpu.VMEM((ATT_HEADS, n_pages, PAGE_SIZE), F32),
                        pltpu.VMEM((ATT_HEADS, HEAD_DIM, PAGE_SIZE), F32),
                        pltpu.SemaphoreType.DMA((2,))],
    )
    return pl.pallas_call(
        functools.partial(_attn_sample_kernel, layer=layer),
        out_shape=jax.ShapeDtypeStruct((n, ATT_HEADS, HEAD_DIM, 1), F32),
        grid_spec=grid_spec,
        compiler_params=_cparams(("arbitrary",)),
        name="attn_sample",
    )(page_table, q4, k_new4, v_new4, sc, thr, cut, ck_t, cv_t)


def _sample_layer(x, mod_s, lw, tabs_s, g_final, final, layer, ck_t, cv_t, cki_t, page_table, pool_prev, conv_prev):
    n = x.shape[0]
    n_pages = page_table.shape[1]
    (acd, q, qi, widx, k, v, ki, v_n, pool_new_t, conv_new_t) = _pre_sample(
        x, mod_s, lw, tabs_s, jnp.swapaxes(pool_prev, 0, 1), jnp.swapaxes(conv_prev, 0, 1))
    sc = _idx_sample(page_table, qi.reshape(n, IDX_HEADS, IDX_DIM), widx.reshape(n, IDX_HEADS, 1),
                     ki.reshape(n, 1, IDX_DIM), cki_t, layer)
    topk = min(TOPK_MAX, (n_pages * PAGE_SIZE + 1) // 4)
    thr, cut = _topk_sample(sc, topk)
    col = lambda a: a.reshape(n, ATT_HEADS, HEAD_DIM, 1)
    ob = _attn_sample(page_table, col(q), col(k), col(v), sc, thr, cut, ck_t, cv_t, layer)
    x_new = _post(x.reshape(1, n, D_MODEL), acd.reshape(1, n, 768), ob.reshape(1, n, 256).astype(BF16),
                  mod_s.reshape(1, 6, n, D_MODEL), lw, g_final, final).reshape(n, D_MODEL)
    return x_new, (k.reshape(n, 1, ATT_HEADS, HEAD_DIM), v.reshape(n, 1, ATT_HEADS, HEAD_DIM),
                   ki.reshape(n, 1, IDX_DIM), v_n.reshape(n, 1, W_GRP),
                   jnp.swapaxes(pool_new_t, 0, 1), jnp.swapaxes(conv_new_t, 0, 1))


def kernel(x_prompt, x_sample, cache_k, cache_v, cache_kidx, state_pool, state_conv, page_table, c_prompt, c_sample,
           w_mod, b_mod, g_attn_norm, g_mlp_norm, w_in, w_out, g_gmlp, w_spatial, b_spatial, w_pool, s_pool, w_dw, b_dw,
           g_conv_norm, b_conv_norm, w_up, w_down, g_final):
    B, S, _ = x_prompt.shape
    n = x_sample.shape[0]
    depth = w_mod.shape[0]
    past = page_table.shape[1] * PAGE_SIZE
    tabs_p = _tables(jnp.arange(S))
    tabs_s = _tables(jnp.full((1,), past, jnp.int32))
    g_fin = g_final.reshape(1, D_MODEL)
    pad = (-B) % SUBLANES
    c_rows = jnp.concatenate([c_prompt, jnp.zeros((pad, D_MODEL), F32), c_sample], axis=0)
    xp, xs = x_prompt, x_sample.reshape(n, D_MODEL)
    ck_t = jnp.transpose(cache_k, (0, 1, 3, 4, 2))
    cv_t = jnp.transpose(cache_v, (0, 1, 3, 4, 2))
    cki_t = jnp.transpose(cache_kidx, (0, 1, 3, 2))
    outs_p, outs_s = [], []
    for l in range(depth):
        lw = _layer_weights(l, w_mod, b_mod, g_attn_norm, g_mlp_norm, w_in, w_out, g_gmlp, w_spatial, b_spatial,
                            w_pool, s_pool, w_dw, b_dw, g_conv_norm, b_conv_norm, w_up, w_down)
        mod = _modulation(c_rows, lw["w_mod"], lw["b_mod"])
        mod_p = mod[:B].reshape(B, 6, D_MODEL)
        mod_s = jnp.swapaxes(mod[B + pad:].reshape(n, 6, D_MODEL), 0, 1)
        final = l == depth - 1
        xp, op = _prompt_layer(xp, mod_p, lw, tabs_p, g_fin, final)
        xs, os_ = _sample_layer(xs, mod_s, lw, tabs_s, g_fin, final, l, ck_t, cv_t, cki_t,
                                page_table, state_pool[l], state_conv[l])
        outs_p.append(op)
        outs_s.append(os_)
    stack = lambda outs, j: jnp.stack([o[j] for o in outs])
    return (xp, xs.reshape(n, 1, D_MODEL),
            stack(outs_p, 0), stack(outs_p, 1), stack(outs_p, 2), stack(outs_p, 3), stack(outs_p, 4),
            stack(outs_s, 0), stack(outs_s, 1), stack(outs_s, 2), stack(outs_s, 3), stack(outs_s, 4), stack(outs_s, 5))
```

```python
import functools

import jax
import jax.numpy as jnp
import numpy as np
from jax import lax
from jax.experimental import pallas as pl
from jax.experimental.pallas import tpu as pltpu

D_MODEL = 1024
W_GRP = 256
HEAD_DIM = 64
A_HEADS = 4
ATT_HEADS = 4
IDX_HEADS = 8
IDX_DIM = 64
CHUNK = 128
TOPK_MAX = 256
PAGE_SIZE = 128
ROPE_THETA = 10000.0
POOL_WINDOWS = (2, 4, 8, 16)
POOL_STATE = 15
CONV_WIDTH = 31
CONV_STATE = 30
CONV_GROUPS = 4
D_FF = 4 * D_MODEL
EPS = 1e-6
IN_SIZES = (256, 256, 256, 256, 256, 512, 64, 8, 256, 256, 256)
D_IN = sum(IN_SIZES)
KIW_OFF = 1792
CX_OFF_RAW = 1864
D_IN_PAD = 2688
CX_OFF, DA_OFF, DG_OFF = 1920, 2176, 2432
WT_Q, WT_QI, WT_V, WT_W, WT_ROWS = 0, 256, 768, 1024, 1040

LANES = 128
SUBLANES = 8
VMEM_LIMIT = 56 * 1024 * 1024

PRE_TM = 256
POST_TM = 512
FF_CHUNK = 1024
QB = 256
KT = 128
KT_UNROLL = 4
BISECT_ITERS = 32
PACK16 = 16
INT16_MIN = -32768
POOL_HALO = 16
CONV_HALO = 32

F32 = jnp.float32
BF16 = jnp.bfloat16
NEG_INF = float("-inf")
KEY_NEG_INF = -2139095041
KEY_POS_INF = 2139095040


def _cparams(sem):
    return pltpu.CompilerParams(dimension_semantics=sem, vmem_limit_bytes=VMEM_LIMIT)


def _dot(a, b):
    return jnp.dot(a, b, preferred_element_type=F32)


def _dot_nt(a, b):
    return lax.dot_general(a, b, (((1,), (1,)), ((), ())), preferred_element_type=F32)


def _group_mean(x, gmat):
    hi = x.astype(BF16)
    lo = (x - hi.astype(F32)).astype(BF16)
    return _dot(hi, gmat) + _dot(lo, gmat)


def _rot_half(x):
    n = x.shape[-1]
    lane = lax.broadcasted_iota(jnp.int32, x.shape, x.ndim - 1)
    fwd = pltpu.roll(x, n - HEAD_DIM // 2, x.ndim - 1)
    bwd = pltpu.roll(x, HEAD_DIM // 2, x.ndim - 1)
    return jnp.where((lane % HEAD_DIM) < HEAD_DIM // 2, fwd, bwd)


def _rope(x, cos, sin_signed):
    reps = x.shape[-1] // LANES
    if reps > 1:
        cos = jnp.concatenate([cos] * reps, axis=-1)
        sin_signed = jnp.concatenate([sin_signed] * reps, axis=-1)
    return x * cos + _rot_half(x) * sin_signed


def _sigmoid(x):
    return 1.0 / (1.0 + jnp.exp(-x))


def _mod_kernel(c_ref, w_ref, b_ref, o_ref):
    c = c_ref[...]
    a = (c * _sigmoid(c)).astype(BF16)
    o_ref[...] = _dot(a, w_ref[...].astype(BF16)) + b_ref[...]


def _modulation(c_rows, w_mod_l, b_mod_l):
    n = c_rows.shape[0]
    ncol = w_mod_l.shape[1]
    tn = 1024
    return pl.pallas_call(
        _mod_kernel,
        out_shape=jax.ShapeDtypeStruct((n, ncol), F32),
        grid=(ncol // tn,),
        in_specs=[
            pl.BlockSpec((n, D_MODEL), lambda j: (0, 0)),
            pl.BlockSpec((D_MODEL, tn), lambda j: (0, j)),
            pl.BlockSpec((1, tn), lambda j: (0, j)),
        ],
        out_specs=pl.BlockSpec((n, tn), lambda j: (0, j)),
        compiler_params=_cparams(("arbitrary",)),
        name="adaln_mod",
    )(c_rows, w_mod_l, b_mod_l.reshape(1, ncol))


def _modulated_norm(x, g, shift, scale):
    ms = jnp.mean(x * x, axis=-1, keepdims=True)
    y = x * lax.rsqrt(ms + EPS) * g
    return y * (1.0 + scale) + shift


def _head_rmsnorm(av, g_row, gmat):
    ms = _group_mean(av * av, gmat)
    return av * lax.rsqrt(ms + EPS) * g_row


def _group_layernorm_silu(y, g_row, b_row, gmat):
    mu = _group_mean(y, gmat)
    d = y - mu
    var = _group_mean(d * d, gmat)
    yn = d * lax.rsqrt(var + EPS) * g_row + b_row
    return yn * _sigmoid(yn)


def _rope_t(x, cos_t, sin_t):
    half = HEAD_DIM // 2
    out = []
    for hh in range(x.shape[0] // HEAD_DIM):
        a = x[hh * HEAD_DIM:hh * HEAD_DIM + half]
        b = x[hh * HEAD_DIM + half:(hh + 1) * HEAD_DIM]
        out.append(a * cos_t - b * sin_t)
        out.append(b * cos_t + a * sin_t)
    return jnp.concatenate(out, axis=0)


def _pre_prompt_kernel(x_ref, mod_ref, gattn_ref, win_ref, wint_ref, ggmlp_ref, wsp_ref, bsp_ref, cos_ref, sin_ref,
                       cost_ref, sint_ref, wpool_ref, spool_ref, wdw_ref, bdw_ref, gcn_ref, bcn_ref, gmat_ref,
                       acd_ref, qt_ref, qit_ref, wt_ref, kbf_ref, kibf_ref, vt_ref,
                       k_ref, v_ref, ki_ref, poolst_ref, convst_ref,
                       pool_buf, conv_buf):
    i = pl.program_id(1)
    tm = x_ref.shape[1]
    gmat = gmat_ref[...]

    @pl.when(i == 0)
    def _():
        pool_buf[0:POOL_HALO, :] = jnp.zeros((POOL_HALO, W_GRP), F32)
        conv_buf[0:CONV_HALO, :] = jnp.zeros((CONV_HALO, W_GRP), F32)

    mod = mod_ref[0]
    h = _modulated_norm(x_ref[0], gattn_ref[...], mod[0:1], mod[1:2]).astype(BF16)

    def proj(off, width):
        return _dot(h, win_ref[:, off:off + width])

    a_u = proj(0, 256)
    v_n = _head_rmsnorm(proj(256, 256), ggmlp_ref[...], gmat)
    v_bf = v_n.astype(BF16)
    lane256 = lax.broadcasted_iota(jnp.int32, (CHUNK, 256), 1)
    rr = lax.broadcasted_iota(jnp.int32, (CHUNK, CHUNK), 0)
    cc = lax.broadcasted_iota(jnp.int32, (CHUNK, CHUNK), 1)
    ws = [jnp.where(cc <= rr, wsp_ref[hh], 0.0).astype(BF16) for hh in range(A_HEADS)]
    gate_chunks = []
    for c in range(tm // CHUNK):
        vc = v_bf[c * CHUNK:(c + 1) * CHUNK, :]
        s = [_dot(ws[hh], vc) for hh in range(A_HEADS)]
        hid = lane256 // HEAD_DIM
        g = jnp.where(hid == 0, s[0], jnp.where(hid == 1, s[1], jnp.where(hid == 2, s[2], s[3])))
        gate_chunks.append(g + bsp_ref[...])
    out_a = a_u * jnp.concatenate(gate_chunks, axis=0)

    cos = cos_ref[...]
    sin = sin_ref[...]
    k = _rope(proj(768, 256), cos, sin)
    k_ref[0] = k
    kbf_ref[0] = k.astype(BF16)
    v_ref[0] = proj(1024, 256)
    ki = _rope(proj(KIW_OFF, LANES), cos, sin)[:, 0:IDX_DIM]
    ki_ref[0] = ki
    kibf_ref[0] = ki.astype(BF16)

    def proj_t(off, width):
        return _dot_nt(wint_ref[off:off + width, :], h)

    cos_t = cost_ref[...]
    sin_t = sint_ref[...]
    qt_ref[0] = (_rope_t(proj_t(WT_Q, 256), cos_t, sin_t) * (HEAD_DIM ** -0.5)).astype(BF16)
    qit_ref[0] = _rope_t(proj_t(WT_QI, 512), cos_t, sin_t).astype(BF16)
    v_t = proj_t(WT_V, 256).astype(BF16)
    for c in range(tm // KT):
        vt_ref[0, c] = v_t[:, c * KT:(c + 1) * KT]
    wt_ref[0] = proj_t(WT_W, 16)[0:IDX_HEADS, :] * ((IDX_HEADS * IDX_DIM) ** -0.5)

    c_x = proj(CX_OFF, 256)
    pool_buf[POOL_HALO:POOL_HALO + tm, :] = c_x
    lane128 = lax.broadcasted_iota(jnp.int32, (tm, LANES), 1)
    tpos = i * tm + lax.broadcasted_iota(jnp.int32, (tm, LANES), 0)
    upper = lane128 >= HEAD_DIM
    pooled = []
    for col, (w_lo, w_hi) in enumerate(((2, 4), (8, 16))):
        cs = slice(col * LANES, (col + 1) * LANES)
        s_lo = pool_buf[POOL_HALO:POOL_HALO + tm, cs]
        for j in range(1, w_lo):
            s_lo = s_lo + pool_buf[POOL_HALO - j:POOL_HALO - j + tm, cs]
        s_hi = pool_buf[POOL_HALO - w_lo:POOL_HALO - w_lo + tm, cs]
        for j in range(w_lo + 1, w_hi):
            s_hi = s_hi + pool_buf[POOL_HALO - j:POOL_HALO - j + tm, cs]
        win = jnp.where(upper, w_hi, w_lo)
        cnt = jnp.minimum(win, tpos + 1).astype(F32)
        pooled.append(jnp.where(upper, s_lo + s_hi, s_lo) / cnt)
    diff = (jnp.concatenate(pooled, axis=-1) - c_x).astype(BF16)
    out_c = _dot(diff, wpool_ref[...]) * spool_ref[...]
    poolst_ref[0] = pool_buf[tm:tm + POOL_HALO, :]
    pool_buf[0:POOL_HALO, :] = pool_buf[tm:tm + POOL_HALO, :]

    glu = proj(DA_OFF, 256) * _sigmoid(proj(DG_OFF, 256))
    conv_buf[CONV_HALO:CONV_HALO + tm, :] = glu
    base = CONV_HALO - CONV_STATE
    y = conv_buf[base:base + tm, :] * wdw_ref[0:1, :]
    for j in range(1, CONV_WIDTH):
        y = y + conv_buf[base + j:base + j + tm, :] * wdw_ref[j:j + 1, :]
    y = y + bdw_ref[...]
    out_d = _group_layernorm_silu(y, gcn_ref[...], bcn_ref[...], gmat)
    convst_ref[0] = conv_buf[tm:tm + CONV_HALO, :]
    conv_buf[0:CONV_HALO, :] = conv_buf[tm:tm + CONV_HALO, :]

    acd_ref[0] = jnp.concatenate([out_a, out_c, out_d], axis=-1).astype(BF16)


def _pre_prompt(x, mod, lw, tabs):
    B, S, _ = x.shape
    tm = min(PRE_TM, S)
    n_t = S // tm
    tok = lambda w: pl.BlockSpec((1, tm, w), lambda b, i: (b, i, 0))
    const = lambda shape: pl.BlockSpec(shape, lambda b, i: tuple(0 for _ in shape))
    in_specs = [
        tok(D_MODEL),
        pl.BlockSpec((1, 6, D_MODEL), lambda b, i: (b, 0, 0)),
        const((1, D_MODEL)),
        const((D_MODEL, D_IN_PAD)),
        const((WT_ROWS, D_MODEL)),
        const((1, W_GRP)),
        const((A_HEADS, CHUNK, CHUNK)),
        const((CHUNK, W_GRP)),
        pl.BlockSpec((tm, LANES), lambda b, i: (i, 0)),
        pl.BlockSpec((tm, LANES), lambda b, i: (i, 0)),
        pl.BlockSpec((HEAD_DIM // 2, tm), lambda b, i: (0, i)),
        pl.BlockSpec((HEAD_DIM // 2, tm), lambda b, i: (0, i)),
        const((W_GRP, W_GRP)),
        const((1, W_GRP)),
        const((32, W_GRP)),
        const((1, W_GRP)),
        const((1, W_GRP)),
        const((1, W_GRP)),
        const((W_GRP, W_GRP)),
    ]
    out_shapes = [
        jax.ShapeDtypeStruct((B, S, 768), BF16),
        jax.ShapeDtypeStruct((B, 256, S), BF16),
        jax.ShapeDtypeStruct((B, 512, S), BF16),
        jax.ShapeDtypeStruct((B, IDX_HEADS, S), F32),
        jax.ShapeDtypeStruct((B, S, 256), BF16),
        jax.ShapeDtypeStruct((B, S, IDX_DIM), BF16),
        jax.ShapeDtypeStruct((B, S // KT, 256, KT), BF16),
        jax.ShapeDtypeStruct((B, S, 256), F32),
        jax.ShapeDtypeStruct((B, S, 256), F32),
        jax.ShapeDtypeStruct((B, S, IDX_DIM), F32),
        jax.ShapeDtypeStruct((B, POOL_HALO, W_GRP), F32),
        jax.ShapeDtypeStruct((B, CONV_HALO, W_GRP), F32),
    ]
    feat = lambda w: pl.BlockSpec((1, w, tm), lambda b, i: (b, 0, i))
    out_specs = [tok(768), feat(256), feat(512), feat(IDX_HEADS), tok(256), tok(IDX_DIM),
                 pl.BlockSpec((1, tm // KT, 256, KT), lambda b, i: (b, i, 0, 0)),
                 tok(256), tok(256), tok(IDX_DIM),
                 pl.BlockSpec((1, POOL_HALO, W_GRP), lambda b, i: (b, 0, 0)),
                 pl.BlockSpec((1, CONV_HALO, W_GRP), lambda b, i: (b, 0, 0))]
    return pl.pallas_call(
        _pre_prompt_kernel,
        out_shape=out_shapes,
        grid=(B, n_t),
        in_specs=in_specs,
        out_specs=out_specs,
        scratch_shapes=[pltpu.VMEM((POOL_HALO + tm, W_GRP), F32), pltpu.VMEM((CONV_HALO + tm, W_GRP), F32)],
        compiler_params=_cparams(("arbitrary", "arbitrary")),
        name="pre_prompt",
    )(x, mod, lw["g_attn"], lw["w_in"], lw["w_in_t"], lw["g_gmlp"], lw["w_spatial"], lw["b_spatial_t"],
      tabs["cos"], tabs["sin"], tabs["cos_t"], tabs["sin_t"],
      lw["w_pool_bd"], lw["s_pool"], lw["w_dw"], lw["b_dw"], lw["g_cn"], lw["b_cn"], tabs["gmat"])


def _key_to_float(key):
    bits = jnp.where(key >= 0, key, key ^ jnp.int32(0x7FFFFFFF))
    return pltpu.bitcast(bits, F32)


def _float_to_key(x):
    bits = pltpu.bitcast(x, jnp.int32)
    return jnp.where(bits >= 0, bits, bits ^ jnp.int32(0x7FFFFFFF))


def _dsa_prompt_kernel(qt_ref, qit_ref, wt_ref, ki_ref, k_ref, vt_ref, o_ref, sc_ref, lg_ref, h16_ref, l16_ref,
                       *, topk):
    i = pl.program_id(1)
    qb = qt_ref.shape[2]
    blk_tiles = qb // KT
    n_real = (i + 1) * blk_tiles
    n_it = (n_real + KT_UNROLL - 1) // KT_UNROLL
    krow = lax.broadcasted_iota(jnp.int32, (KT, qb), 0)
    q_pos = i * qb + lax.broadcasted_iota(jnp.int32, (KT, qb), 1)
    q_pos1 = i * qb + lax.broadcasted_iota(jnp.int32, (1, qb), 1)

    def fold_sum(x):
        return jnp.sum(x.reshape(KT // SUBLANES, SUBLANES, qb), axis=0)

    def fold_max(x):
        return jnp.max(x.reshape(KT // SUBLANES, SUBLANES, qb), axis=0)

    def fold16(x):
        parts = [x[j * PACK16:(j + 1) * PACK16] for j in range(KT // PACK16)]
        while len(parts) > 1:
            parts = [a + b for a, b in zip(parts[0::2], parts[1::2])]
        return parts[0]

    def tile_of(it, u):
        return it * KT_UNROLL + u

    def key_rows(kt, n=KT):
        return pl.ds(pl.multiple_of(kt * KT, KT), n)

    w_rows = [wt_ref[0, hh:hh + 1, :] for hh in range(IDX_HEADS)]

    def score_tile(kt, diagonal):
        kc = ki_ref[0, key_rows(kt), :]
        acc = jnp.zeros((KT, qb), F32)
        for hh in range(IDX_HEADS):
            d = _dot(kc, qit_ref[0, hh * IDX_DIM:(hh + 1) * IDX_DIM, :])
            acc = acc + w_rows[hh] * jnp.maximum(d, 0.0)
        hi16 = _float_to_key(acc) >> 16
        if diagonal:
            causal = kt * KT + krow <= q_pos
            acc = jnp.where(causal, acc, NEG_INF)
            hi16 = jnp.where(causal, hi16, INT16_MIN)
        sc_ref[kt] = acc
        h16_ref[kt] = hi16.astype(jnp.int16)

    def score_body(it, carry):
        for u in range(blk_tiles):
            score_tile(it * blk_tiles + u, False)
        return carry

    lax.fori_loop(0, i, score_body, 0)
    for u in range(blk_tiles):
        score_tile(i * blk_tiles + u, True)

    @pl.when(n_it * KT_UNROLL > n_real)
    def _():
        for u in range(blk_tiles):
            sc_ref[n_real + u] = jnp.full((KT, qb), NEG_INF, F32)
            h16_ref[n_real + u] = jnp.full((KT, qb), INT16_MIN, jnp.int16)

    def count16(ref, t):
        t16 = t.astype(jnp.int16)

        def body(it, acc):
            for u in range(KT_UNROLL):
                hit = jnp.where(ref[tile_of(it, u)] >= t16, jnp.int16(1), jnp.int16(0))
                acc = acc + fold16(hit)
            return acc
        acc = lax.fori_loop(0, n_it, body, jnp.zeros((PACK16, qb), jnp.int16))
        return jnp.sum(acc.astype(jnp.int32), axis=0, keepdims=True)

    def count_where(pred):
        def body(it, acc):
            for u in range(KT_UNROLL):
                kt = tile_of(it, u)
                acc = acc + fold_sum(jnp.where(pred(sc_ref[kt], kt), 1.0, 0.0))
            return acc
        acc = lax.fori_loop(0, n_it, body, jnp.zeros((SUBLANES, qb), F32))
        return jnp.sum(acc, axis=0, keepdims=True).astype(jnp.int32)

    k_row = jnp.minimum(topk, q_pos1 + 1)

    def upper_body(_, carry):
        lo, hi, cnt_hi = carry
        mid = (lo + hi) >> 1
        cnt = count16(h16_ref, mid)
        ge = cnt >= k_row
        return jnp.where(ge, mid, lo), jnp.where(ge, hi, mid), jnp.where(ge, cnt_hi, cnt)

    full = lambda v: jnp.full((1, qb), v, jnp.int32)
    t_hi, _, n_above = lax.fori_loop(0, 16, upper_body, (full(INT16_MIN + 1), full(-INT16_MIN), full(0)))

    def lower_prep(it, carry):
        for u in range(KT_UNROLL):
            kt = tile_of(it, u)
            s = sc_ref[kt]
            key = _float_to_key(s)
            low = jnp.where((key >> 16) == t_hi, (key & 0xFFFF) >> 1, -1)
            l16_ref[kt] = jnp.where(s == NEG_INF, -1, low).astype(jnp.int16)
        return carry

    lax.fori_loop(0, n_it, lower_prep, 0)
    k_low = k_row - n_above

    def lower_body(_, carry):
        lo, hi = carry
        mid = (lo + hi) >> 1
        ge = count16(l16_ref, mid) >= k_low
        return jnp.where(ge, mid, lo), jnp.where(ge, hi, mid)

    t_lo, _ = lax.fori_loop(0, 15, lower_body, (full(0), full(-INT16_MIN)))
    key_even = (t_hi << 16) | (t_lo << 1)
    count_ge_key = lambda key: count_where(lambda s, kt: s >= _key_to_float(key))
    cnt_odd = count_ge_key(key_even + 1)
    odd_ok = cnt_odd >= k_row
    key_thr = jnp.where(odd_ok, key_even + 1, key_even)
    cnt_probe = count_ge_key(jnp.where(odd_ok, key_thr + 1, key_thr))
    n_ge = jnp.where(odd_ok, cnt_odd, cnt_probe)
    n_next = jnp.where(odd_ok, cnt_probe, cnt_odd)

    consistent = (n_ge >= k_row) & (n_next < k_row)

    def compare_bisect(_):
        def body(_, carry):
            lo, hi = carry
            mid = (lo >> 1) + (hi >> 1) + (lo & hi & 1)
            ge = count_ge_key(mid) >= k_row
            return jnp.where(ge, mid, lo), jnp.where(ge, hi, mid)
        lo, _ = lax.fori_loop(0, BISECT_ITERS, body, (full(KEY_NEG_INF + 1), full(KEY_POS_INF + 1)))
        return lo, count_ge_key(lo)

    key_thr, n_ge = lax.cond(jnp.max(jnp.where(consistent, 0.0, 1.0)) > 0.0, compare_bisect,
                             lambda _: (key_thr, n_ge), 0)
    thr = _key_to_float(key_thr)

    def selected(s, kt, cut):
        return (s > thr) | ((s == thr) & (kt * KT + krow <= cut))

    n_keys = n_it * KT_UNROLL * KT
    any_excess = jnp.max(jnp.where(n_ge > k_row, 1.0, 0.0)) > 0.0

    def tie_cut(_):
        def body(_, carry):
            lo_c, hi_c = carry
            mid = (lo_c + hi_c) >> 1
            ge = count_where(lambda s, kt: selected(s, kt, mid)) >= k_row
            return jnp.where(ge, lo_c, mid), jnp.where(ge, mid, hi_c)
        lo_c = jnp.full((1, qb), -1, jnp.int32)
        hi_c = jnp.full((1, qb), 1, jnp.int32) * (n_keys - 1)
        _, hi_c = lax.fori_loop(0, 13, body, (lo_c, hi_c))
        return hi_c

    cut = lax.cond(any_excess, tie_cut, lambda _: jnp.full((1, qb), 1, jnp.int32) * n_keys, 0)

    def bias_body(it, carry):
        for u in range(KT_UNROLL):
            kt = tile_of(it, u)
            sc_ref[kt] = jnp.where(selected(sc_ref[kt], kt, cut), 0.0, NEG_INF)
        return carry

    lax.fori_loop(0, n_it, bias_body, 0)

    pair_row = lax.broadcasted_iota(jnp.int32, (2 * HEAD_DIM, qb), 0)
    o_heads = []
    for p in range(ATT_HEADS // 2):
        cs = slice(p * LANES, (p + 1) * LANES)
        slab = qt_ref[0, cs, :].astype(F32)
        q2 = jnp.concatenate([jnp.where(pair_row < HEAD_DIM, slab, 0.0),
                              jnp.where(pair_row >= HEAD_DIM, slab, 0.0)], axis=1).astype(BF16)

        def logit_body(it, mruns):
            mruns = list(mruns)
            lg4 = _dot(k_ref[0, key_rows(it * KT_UNROLL, KT_UNROLL * KT), cs], q2)
            for u in range(KT_UNROLL):
                kt = tile_of(it, u)
                for half in range(2):
                    lgt = lg4[u * KT:(u + 1) * KT, half * qb:(half + 1) * qb] + sc_ref[kt]
                    lg_ref[half, kt] = lgt
                    mruns[half] = jnp.maximum(mruns[half], fold_max(lgt))
            return tuple(mruns)

        neg = jnp.full((SUBLANES, qb), NEG_INF, F32)
        mruns = lax.fori_loop(0, n_it, logit_body, (neg, neg))
        mx = [jnp.max(m, axis=0, keepdims=True) for m in mruns]

        def pv_body(it, carry):
            lruns, accs = list(carry[0]), list(carry[1])
            for half in range(2):
                rows = slice((2 * p + half) * HEAD_DIM, (2 * p + half + 1) * HEAD_DIM)
                pes = []
                for u in range(KT_UNROLL):
                    pe = jnp.exp(lg_ref[half, tile_of(it, u)] - mx[half])
                    lruns[half] = lruns[half] + fold_sum(pe)
                    pes.append(pe.astype(BF16))
                v4 = jnp.concatenate([vt_ref[0, tile_of(it, u), rows, :] for u in range(KT_UNROLL)], axis=1)
                accs[half] = accs[half] + _dot(v4, jnp.concatenate(pes, axis=0))
            return tuple(lruns), tuple(accs)

        zl = jnp.zeros((SUBLANES, qb), F32)
        za = jnp.zeros((HEAD_DIM, qb), F32)
        lruns, accs = lax.fori_loop(0, n_it, pv_body, ((zl, zl), (za, za)))
        for half in range(2):
            o_heads.append(accs[half] / jnp.sum(lruns[half], axis=0, keepdims=True))
    o_ref[0] = jnp.concatenate(o_heads, axis=0).T.astype(BF16)


def _dsa_prompt(qt, qit, wt, kibf, kbf, vt):
    B, _, S = qt.shape
    qb = min(QB, S)
    n_kt = S // KT
    assert S % (KT * KT_UNROLL) == 0 and qb % KT == 0
    feat = lambda w: pl.BlockSpec((1, w, qb), lambda b, i: (b, 0, i))
    full = lambda w: pl.BlockSpec((1, S, w), lambda b, i: (b, 0, 0))
    return pl.pallas_call(
        functools.partial(_dsa_prompt_kernel, topk=min(TOPK_MAX, S // 4)),
        out_shape=jax.ShapeDtypeStruct((B, S, 256), BF16),
        grid=(B, S // qb),
        in_specs=[feat(256), feat(512), feat(IDX_HEADS), full(IDX_DIM), full(256),
                  pl.BlockSpec((1, n_kt, 256, KT), lambda b, i: (b, 0, 0, 0))],
        out_specs=pl.BlockSpec((1, qb, 256), lambda b, i: (b, i, 0)),
        scratch_shapes=[pltpu.VMEM((n_kt, KT, qb), F32), pltpu.VMEM((2, n_kt, KT, qb), F32),
                        pltpu.VMEM((n_kt, KT, qb), jnp.int16), pltpu.VMEM((n_kt, KT, qb), jnp.int16)],
        compiler_params=_cparams(("arbitrary", "arbitrary")),
        name="dsa_prompt",
    )(qt, qit, wt, kibf, kbf, vt)


def _post_kernel(x_ref, acd_ref, ob_ref, mod_ref, gmlp_ref, woacd_ref, wob_ref, wup_ref, wdn_ref, gfin_ref, o_ref,
                 *, final):
    mod = mod_ref[0]
    mixed = _dot(acd_ref[0], woacd_ref[...]) + _dot(ob_ref[0], wob_ref[...])
    x1 = x_ref[0] + mod[2] * mixed
    h2 = _modulated_norm(x1, gmlp_ref[...], mod[3], mod[4]).astype(BF16)
    acc = jnp.zeros_like(x1)
    for c in range(D_FF // FF_CHUNK):
        u = jnp.maximum(_dot(h2, wup_ref[:, c * FF_CHUNK:(c + 1) * FF_CHUNK]), 0.0)
        acc = acc + _dot((u * u).astype(BF16), wdn_ref[c * FF_CHUNK:(c + 1) * FF_CHUNK, :])
    x2 = x1 + mod[5] * acc
    if final:
        ms = jnp.mean(x2 * x2, axis=-1, keepdims=True)
        x2 = x2 * lax.rsqrt(ms + EPS) * gfin_ref[...]
    o_ref[0] = x2


def _post(x, acd, ob, mod, lw, g_final, final):
    G, T, _ = x.shape
    tm = min(POST_TM, T)
    R = mod.shape[2]
    tok = lambda w: pl.BlockSpec((1, tm, w), lambda b, i: (b, i, 0))
    const = lambda shape: pl.BlockSpec(shape, lambda b, i: tuple(0 for _ in shape), pipeline_mode=pl.Buffered(1))
    if R == 1:
        mod_spec = pl.BlockSpec((1, 6, 1, D_MODEL), lambda b, i: (b, 0, 0, 0))
    else:
        mod_spec = pl.BlockSpec((1, 6, tm, D_MODEL), lambda b, i: (b, 0, i, 0))
    return pl.pallas_call(
        functools.partial(_post_kernel, final=final),
        out_shape=jax.ShapeDtypeStruct((G, T, D_MODEL), F32),
        grid=(G, T // tm),
        in_specs=[tok(D_MODEL), tok(768), tok(256), mod_spec, const((1, D_MODEL)),
                  const((768, D_MODEL)), const((256, D_MODEL)), const((D_MODEL, D_FF)), const((D_FF, D_MODEL)),
                  const((1, D_MODEL))],
        out_specs=tok(D_MODEL),
        compiler_params=_cparams(("arbitrary", "arbitrary")),
        name="post_final" if final else "post",
    )(x, acd, ob, mod, lw["g_mlp"], lw["w_out_acd"], lw["w_out_b"], lw["w_up"], lw["w_down"], g_final)


def _layer_weights(l, w_mod, b_mod, g_attn_norm, g_mlp_norm, w_in, w_out, g_gmlp, w_spatial, b_spatial,
                   w_pool, s_pool, w_dw, b_dw, g_conv_norm, b_conv_norm, w_up, w_down):
    w_in_l = w_in[l]
    w_in_pad = jnp.concatenate(
        [w_in_l[:, :CX_OFF_RAW], jnp.zeros((D_MODEL, CX_OFF - CX_OFF_RAW), w_in_l.dtype), w_in_l[:, CX_OFF_RAW:]],
        axis=1).astype(BF16)
    w_pool_bd = jnp.zeros((W_GRP, W_GRP), F32)
    for g in range(len(POOL_WINDOWS)):
        w_pool_bd = w_pool_bd.at[g * 64:(g + 1) * 64, g * 64:(g + 1) * 64].set(w_pool[l, g])
    w_out_l = w_out[l]
    w_in_t = jnp.concatenate(
        [w_in_l[:, 512:768], w_in_l[:, 1280:1792], w_in_l[:, 1024:1280], w_in_l[:, 1856:1864],
         jnp.zeros((D_MODEL, WT_ROWS - WT_W - IDX_HEADS), w_in_l.dtype)], axis=1).T.astype(BF16)
    return dict(
        w_in_t=w_in_t,
        w_mod=w_mod[l], b_mod=b_mod[l],
        g_attn=g_attn_norm[l].reshape(1, D_MODEL), g_mlp=g_mlp_norm[l].reshape(1, D_MODEL),
        w_in=w_in_pad,
        g_gmlp=g_gmlp[l].reshape(1, W_GRP),
        w_spatial=w_spatial[l],
        b_spatial_t=jnp.repeat(b_spatial[l].T, HEAD_DIM, axis=1),
        w_sp0=jnp.repeat(w_spatial[l, :, 0, 0], HEAD_DIM).reshape(1, W_GRP),
        b_sp0=jnp.repeat(b_spatial[l, :, 0], HEAD_DIM).reshape(1, W_GRP),
        w_pool_bd=w_pool_bd.astype(BF16), s_pool=s_pool[l].reshape(1, W_GRP),
        w_dw=jnp.concatenate([w_dw[l], jnp.zeros((1, W_GRP), F32)], axis=0), b_dw=b_dw[l].reshape(1, W_GRP),
        g_cn=g_conv_norm[l].reshape(1, W_GRP), b_cn=b_conv_norm[l].reshape(1, W_GRP),
        w_out_acd=jnp.concatenate([w_out_l[0:256], w_out_l[512:1024]], axis=0).astype(BF16),
        w_out_b=w_out_l[256:512].astype(BF16),
        w_up=w_up[l].astype(BF16), w_down=w_down[l].astype(BF16),
    )


def _tables(pos):
    half = HEAD_DIM // 2
    freqs = ROPE_THETA ** (-jnp.arange(half, dtype=F32) / half)
    ang = pos.astype(F32)[:, None] * freqs[None, :]
    cos = jnp.tile(jnp.cos(ang), (1, LANES // half))
    sin = jnp.sin(ang)
    sin_signed = jnp.tile(jnp.concatenate([-sin, sin], axis=1), (1, LANES // HEAD_DIM))
    gid = jnp.arange(W_GRP) // HEAD_DIM
    gmat = jnp.where(gid[:, None] == gid[None, :], 1.0 / HEAD_DIM, 0.0).astype(BF16)
    return dict(cos=cos, sin=sin_signed, gmat=gmat, cos_t=jnp.cos(ang).T, sin_t=sin.T)


def _prompt_layer(x, mod_p, lw, tabs, g_final, final):
    B, S, _ = x.shape
    (acd, qt, qit, wt, kbf, kibf, vt, k, v, ki, poolst, convst) = _pre_prompt(x, mod_p, lw, tabs)
    ob = _dsa_prompt(qt, qit, wt, kibf, kbf, vt)
    x_new = _post(x, acd, ob, mod_p.reshape(B, 6, 1, D_MODEL), lw, g_final, final)
    return x_new, (k.reshape(B, S, ATT_HEADS, HEAD_DIM), v.reshape(B, S, ATT_HEADS, HEAD_DIM), ki,
                   poolst[:, POOL_HALO - POOL_STATE:], convst[:, CONV_HALO - CONV_STATE:])


def _pre_sample_kernel(x_ref, mod_ref, gattn_ref, win_ref, ggmlp_ref, wsp0_ref, bsp0_ref, cos_ref, sin_ref,
                       wpool_ref, spool_ref, wdw_ref, bdw_ref, gcn_ref, bcn_ref, gmat_ref, poolprev_ref, convprev_ref,
                       acd_ref, q_ref, qi_ref, widx_ref, k_ref, v_ref, ki_ref, vn_ref, poolnew_ref, convnew_ref):
    gmat = gmat_ref[...]
    h = _modulated_norm(x_ref[...], gattn_ref[...], mod_ref[0], mod_ref[1]).astype(BF16)

    def proj(off, width):
        return _dot(h, win_ref[:, off:off + width])

    a_u = proj(0, 256)
    v_n = _head_rmsnorm(proj(256, 256), ggmlp_ref[...], gmat)
    vn_ref[...] = v_n
    out_a = a_u * (wsp0_ref[...] * v_n + bsp0_ref[...])

    cos = cos_ref[...]
    sin = sin_ref[...]
    q_ref[...] = _rope(proj(512, 256), cos, sin) * (HEAD_DIM ** -0.5)
    k_ref[...] = _rope(proj(768, 256), cos, sin)
    v_ref[...] = proj(1024, 256)
    qi_ref[...] = _rope(proj(1280, 512), cos, sin).astype(BF16)
    kiw = proj(KIW_OFF, LANES)
    ki_ref[...] = _rope(kiw, cos, sin)[:, 0:IDX_DIM]
    widx_ref[...] = kiw[:, IDX_DIM:IDX_DIM + IDX_HEADS] * ((IDX_HEADS * IDX_DIM) ** -0.5)

    c_x = proj(CX_OFF, 256)
    sums = {}
    run = c_x
    for j in range(1, max(POOL_WINDOWS)):
        run = run + poolprev_ref[POOL_STATE - j]
        if j + 1 in POOL_WINDOWS:
            sums[j + 1] = run * (1.0 / (j + 1))
    gid = lax.broadcasted_iota(jnp.int32, c_x.shape, 1) // HEAD_DIM
    pooled = jnp.where(gid == 0, sums[2], jnp.where(gid == 1, sums[4], jnp.where(gid == 2, sums[8], sums[16])))
    diff = (pooled - c_x).astype(BF16)
    out_c = _dot(diff, wpool_ref[...]) * spool_ref[...]
    for r in range(POOL_STATE - 1):
        poolnew_ref[r] = poolprev_ref[r + 1]
    poolnew_ref[POOL_STATE - 1] = c_x

    glu = proj(DA_OFF, 256) * _sigmoid(proj(DG_OFF, 256))
    y = glu * wdw_ref[CONV_STATE:CONV_STATE + 1, :]
    for j in range(CONV_STATE):
        y = y + convprev_ref[j] * wdw_ref[j:j + 1, :]
    y = y + bdw_ref[...]
    out_d = _group_layernorm_silu(y, gcn_ref[...], bcn_ref[...], gmat)
    for r in range(CONV_STATE - 1):
        convnew_ref[r] = convprev_ref[r + 1]
    convnew_ref[CONV_STATE - 1] = glu

    acd_ref[...] = jnp.concatenate([out_a, out_c, out_d], axis=-1).astype(BF16)


def _pre_sample(x, mod_s, lw, tabs_s, pool_prev_t, conv_prev_t):
    n = x.shape[0]
    out_shapes = [
        jax.ShapeDtypeStruct((n, 768), BF16),
        jax.ShapeDtypeStruct((n, 256), F32),
        jax.ShapeDtypeStruct((n, 512), BF16),
        jax.ShapeDtypeStruct((n, IDX_HEADS), F32),
        jax.ShapeDtypeStruct((n, 256), F32),
        jax.ShapeDtypeStruct((n, 256), F32),
        jax.ShapeDtypeStruct((n, IDX_DIM), F32),
        jax.ShapeDtypeStruct((n, 256), F32),
        jax.ShapeDtypeStruct((POOL_STATE, n, W_GRP), F32),
        jax.ShapeDtypeStruct((CONV_STATE, n, W_GRP), F32),
    ]
    return pl.pallas_call(
        _pre_sample_kernel,
        out_shape=out_shapes,
        compiler_params=pltpu.CompilerParams(vmem_limit_bytes=VMEM_LIMIT),
        name="pre_sample",
    )(x, mod_s, lw["g_attn"], lw["w_in"], lw["g_gmlp"], lw["w_sp0"], lw["b_sp0"], tabs_s["cos"], tabs_s["sin"],
      lw["w_pool_bd"], lw["s_pool"], lw["w_dw"], lw["b_dw"], lw["g_cn"], lw["b_cn"], tabs_s["gmat"],
      pool_prev_t, conv_prev_t)


SC_CHUNKS = 17
PAGES_PER_CHUNK = 8
NEW_CHUNK = 16
ATT_PAGES = 16
ATT_SLOTS = 3


def _idx_sample_kernel(pt_ref, qi_ref, w_ref, kinew_ref, cki_ref, sc_ref, kbuf, sems, *, layer):
    b = pl.program_id(0)
    nb = pl.num_programs(0)
    n_pages = kbuf.shape[1]

    def page_copy(bb, p, slot):
        return pltpu.make_async_copy(cki_ref.at[layer, pt_ref[bb, p]], kbuf.at[slot, p], sems.at[slot])

    def start_all(bb, slot):
        def body(p, c):
            page_copy(bb, p, slot).start()
            return c
        lax.fori_loop(0, n_pages, body, 0)

    slot = b % 2

    @pl.when(b == 0)
    def _():
        start_all(0, 0)

    @pl.when(b + 1 < nb)
    def _():
        start_all(b + 1, 1 - slot)

    def wait_body(p, c):
        page_copy(b, p, slot).wait()
        return c
    lax.fori_loop(0, n_pages, wait_body, 0)

    qi = qi_ref[0]
    w = w_ref[0]

    def chunk_body(c, carry):
        tiles = [kbuf[slot, c * PAGES_PER_CHUNK + j].astype(BF16) for j in range(PAGES_PER_CHUNK)]
        d = jnp.maximum(_dot(qi, jnp.concatenate(tiles, axis=1)), 0.0) * w
        s = jnp.sum(d, axis=0, keepdims=True)
        for j in range(PAGES_PER_CHUNK):
            sc_ref[0, c, j:j + 1, :] = s[:, j * PAGE_SIZE:(j + 1) * PAGE_SIZE]
        return carry

    lax.fori_loop(0, n_pages // PAGES_PER_CHUNK, chunk_body, 0)

    k_new = kinew_ref[0].astype(BF16).astype(F32)
    d_new = jnp.sum(qi.astype(F32) * k_new, axis=1, keepdims=True)
    s_new = jnp.sum(jnp.maximum(d_new, 0.0) * w, axis=0, keepdims=True)
    lane = lax.broadcasted_iota(jnp.int32, (1, PAGE_SIZE), 1)
    sc_ref[0, NEW_CHUNK] = jnp.full((SUBLANES, PAGE_SIZE), NEG_INF, F32)
    sc_ref[0, NEW_CHUNK, 0:1, :] = jnp.where(lane == 0, s_new, NEG_INF)


def _idx_sample(page_table, qi3, w3, ki_new3, cki, layer):
    n, n_pages = page_table.shape
    assert n_pages == (NEW_CHUNK * PAGES_PER_CHUNK)
    grid_spec = pltpu.PrefetchScalarGridSpec(
        num_scalar_prefetch=1,
        grid=(n,),
        in_specs=[
            pl.BlockSpec((1, IDX_HEADS, IDX_DIM), lambda b, pt: (b, 0, 0)),
            pl.BlockSpec((1, IDX_HEADS, 1), lambda b, pt: (b, 0, 0)),
            pl.BlockSpec((1, 1, IDX_DIM), lambda b, pt: (b, 0, 0)),
            pl.BlockSpec(memory_space=pl.ANY),
        ],
        out_specs=pl.BlockSpec((1, SC_CHUNKS, SUBLANES, PAGE_SIZE), lambda b, pt: (b, 0, 0, 0)),
        scratch_shapes=[pltpu.VMEM((2, n_pages, IDX_DIM, PAGE_SIZE), F32), pltpu.SemaphoreType.DMA((2,))],
    )
    return pl.pallas_call(
        functools.partial(_idx_sample_kernel, layer=layer),
        out_shape=jax.ShapeDtypeStruct((n, SC_CHUNKS, SUBLANES, PAGE_SIZE), F32),
        grid_spec=grid_spec,
        compiler_params=_cparams(("arbitrary",)),
        name="idx_sample",
    )(page_table, qi3, w3, ki_new3, cki)


def _topk_sample_kernel(sc_ref, thr_ref, cut_ref, *, topk):
    n = sc_ref.shape[0]
    shape4 = sc_ref.shape

    def total(x):
        x = jnp.sum(x, axis=1, keepdims=True)
        x = jnp.sum(x, axis=2, keepdims=True)
        return jnp.sum(x, axis=3, keepdims=True)

    kf = float(topk)

    def bisect_body(_, carry):
        lo, hi = carry
        mid = (lo >> 1) + (hi >> 1) + (lo & hi & 1)
        cnt = total(jnp.where(sc_ref[...] >= _key_to_float(mid), 1.0, 0.0))
        ge = cnt >= kf
        return jnp.where(ge, mid, lo), jnp.where(ge, hi, mid)

    lo0 = jnp.full((n, 1, 1, 1), KEY_NEG_INF + 1, jnp.int32)
    hi0 = jnp.full((n, 1, 1, 1), KEY_POS_INF + 1, jnp.int32)
    lo, _ = lax.fori_loop(0, BISECT_ITERS, bisect_body, (lo0, hi0))
    thr = _key_to_float(lo)

    kidx = (lax.broadcasted_iota(jnp.int32, shape4, 1) * (SUBLANES * PAGE_SIZE)
            + lax.broadcasted_iota(jnp.int32, shape4, 2) * PAGE_SIZE
            + lax.broadcasted_iota(jnp.int32, shape4, 3))

    def count_sel(cut):
        s = sc_ref[...]
        sel = (s > thr) | ((s == thr) & (kidx <= cut))
        return total(jnp.where(sel, 1.0, 0.0))

    n_keys = SC_CHUNKS * SUBLANES * PAGE_SIZE

    def cut_body(_, carry):
        lo_c, hi_c = carry
        mid = (lo_c + hi_c) >> 1
        ge = count_sel(mid) >= kf
        return jnp.where(ge, lo_c, mid), jnp.where(ge, mid, hi_c)

    lo_c = jnp.full((n, 1, 1, 1), -1, jnp.int32)
    hi_c = jnp.full((n, 1, 1, 1), n_keys - 1, jnp.int32)
    _, cut = lax.fori_loop(0, 15, cut_body, (lo_c, hi_c))
    thr_ref[...] = thr
    cut_ref[...] = cut


def _topk_sample(sc, topk):
    n = sc.shape[0]
    return pl.pallas_call(
        functools.partial(_topk_sample_kernel, topk=topk),
        out_shape=[jax.ShapeDtypeStruct((n, 1, 1, 1), F32), jax.ShapeDtypeStruct((n, 1, 1, 1), jnp.int32)],
        compiler_params=pltpu.CompilerParams(vmem_limit_bytes=VMEM_LIMIT),
        name="topk_sample",
    )(sc)


def _attn_sample_kernel(pt_ref, q_ref, knew_ref, vnew_ref, sc_ref, thr_ref, cut_ref, ck_ref, cv_ref, o_ref,
                        buf, lg_ref, qb_ref, sems, *, layer):
    b = pl.program_id(0)
    nb = pl.num_programs(0)
    n_pages = lg_ref.shape[1]
    n_ch = n_pages // ATT_PAGES

    def chunk_copy(src_ref, bb, u, j, slot):
        return pltpu.make_async_copy(src_ref.at[layer, pt_ref[bb, u * ATT_PAGES + j]], buf.at[slot, j], sems.at[slot])

    def start_chunk(src_ref, bb, u, slot):
        for j in range(ATT_PAGES):
            chunk_copy(src_ref, bb, u, j, slot).start()

    def wait_chunk(src_ref, bb, u, slot):
        for j in range(ATT_PAGES):
            chunk_copy(src_ref, bb, u, j, slot).wait()

    per_seq = 2 * n_ch
    n_flat = nb * per_seq

    def start_flat(g):
        bb = g // per_seq
        c = g % per_seq
        slot = g % ATT_SLOTS

        @pl.when(c < n_ch)
        def _():
            start_chunk(ck_ref, bb, c, slot)

        @pl.when(c >= n_ch)
        def _():
            start_chunk(cv_ref, bb, c - n_ch, slot)

    def prefetch(g):
        @pl.when(g + ATT_SLOTS - 1 < n_flat)
        def _():
            start_flat(g + ATT_SLOTS - 1)

    @pl.when(b == 0)
    def _():
        for g0 in range(ATT_SLOTS - 1):
            start_flat(jnp.int32(g0))

    for hh in range(ATT_HEADS):
        qb_ref[hh] = jnp.broadcast_to(q_ref[0, hh], (HEAD_DIM, PAGE_SIZE))

    def k_body(u, carry):
        g = b * per_seq + u
        slot = g % ATT_SLOTS
        wait_chunk(ck_ref, b, u, slot)
        prefetch(g)
        for j in range(ATT_PAGES):
            for hh in range(ATT_HEADS):
                lg = jnp.sum(buf[slot, j, hh] * qb_ref[hh], axis=0, keepdims=True)
                lg_ref[hh, pl.ds(u * ATT_PAGES + j, 1), :] = lg
        return carry

    lax.fori_loop(0, n_ch, k_body, 0)

    rows = SC_CHUNKS * SUBLANES
    s2 = sc_ref[0].reshape(rows, PAGE_SIZE)
    thr = thr_ref[0, 0]
    cut = cut_ref[0, 0]
    kidx = (lax.broadcasted_iota(jnp.int32, (rows, PAGE_SIZE), 0) * PAGE_SIZE
            + lax.broadcasted_iota(jnp.int32, (rows, PAGE_SIZE), 1))
    sel = jnp.where((s2 > thr) | ((s2 == thr) & (kidx <= cut)), 1.0, 0.0)
    sel_past = sel[0:n_pages, :] > 0.0
    sel_new = sel[n_pages:n_pages + 1, 0:1] > 0.0
    e_new, den = [], []
    for hh in range(ATT_HEADS):
        lgm = jnp.where(sel_past, lg_ref[hh], NEG_INF)
        l_new = jnp.sum(knew_ref[0, hh] * q_ref[0, hh], axis=0, keepdims=True)
        l_new = jnp.where(sel_new, l_new, NEG_INF)
        m = jnp.maximum(jnp.max(jnp.max(lgm, axis=1, keepdims=True), axis=0, keepdims=True), l_new)
        e = jnp.exp(lgm - m)
        lg_ref[hh] = e
        e_new.append(jnp.exp(l_new - m))
        den.append(jnp.sum(jnp.sum(e, axis=1, keepdims=True), axis=0, keepdims=True) + e_new[hh])

    def v_body(u, accs):
        g = b * per_seq + n_ch + u
        slot = g % ATT_SLOTS
        wait_chunk(cv_ref, b, u, slot)
        prefetch(g)
        accs = list(accs)
        for j in range(ATT_PAGES):
            for hh in range(ATT_HEADS):
                p_row = lg_ref[hh, pl.ds(u * ATT_PAGES + j, 1), :]
                accs[hh] = accs[hh] + buf[slot, j, hh] * p_row
        return tuple(accs)

    accs = lax.fori_loop(0, n_ch, v_body, tuple(jnp.zeros((HEAD_DIM, PAGE_SIZE), F32) for _ in range(ATT_HEADS)))
    for hh in range(ATT_HEADS):
        o = jnp.sum(accs[hh], axis=1, keepdims=True) + e_new[hh] * vnew_ref[0, hh]
        o_ref[0, hh] = o / den[hh]


def _attn_sample(page_table, q4, k_new4, v_new4, sc, thr, cut, ck_t, cv_t, layer):
    n, n_pages = page_table.shape
    col_blk = pl.BlockSpec((1, ATT_HEADS, HEAD_DIM, 1), lambda b, pt: (b, 0, 0, 0))
    one_blk = pl.BlockSpec((1, 1, 1, 1), lambda b, pt: (b, 0, 0, 0))
    grid_spec = pltpu.PrefetchScalarGridSpec(
        num_scalar_prefetch=1,
        grid=(n,),
        in_specs=[col_blk, col_blk, col_blk,
                  pl.BlockSpec((1, SC_CHUNKS, SUBLANES, PAGE_SIZE), lambda b, pt: (b, 0, 0, 0)),
                  one_blk, one_blk, pl.BlockSpec(memory_space=pl.ANY), pl.BlockSpec(memory_space=pl.ANY)],
        out_specs=col_blk,
        scratch_shapes=[pltpu.VMEM((ATT_SLOTS, ATT_PAGES, ATT_HEADS, HEAD_DIM, PAGE_SIZE), F32),
                        pltpu.VMEM((ATT_HEADS, n_pages, PAGE_SIZE), F32),
                        pltpu.VMEM((ATT_HEADS, HEAD_DIM, PAGE_SIZE), F32),
                        pltpu.SemaphoreType.DMA((ATT_SLOTS,))],
    )
    return pl.pallas_call(
        functools.partial(_attn_sample_kernel, layer=layer),
        out_shape=jax.ShapeDtypeStruct((n, ATT_HEADS, HEAD_DIM, 1), F32),
        grid_spec=grid_spec,
        compiler_params=_cparams(("arbitrary",)),
        name="attn_sample",
    )(page_table, q4, k_new4, v_new4, sc, thr, cut, ck_t, cv_t)


def _sample_layer(x, mod_s, lw, tabs_s, g_final, final, layer, ck_t, cv_t, cki_t, page_table, pool_prev, conv_prev):
    n = x.shape[0]
    n_pages = page_table.shape[1]
    (acd, q, qi, widx, k, v, ki, v_n, pool_new_t, conv_new_t) = _pre_sample(
        x, mod_s, lw, tabs_s, jnp.swapaxes(pool_prev, 0, 1), jnp.swapaxes(conv_prev, 0, 1))
    sc = _idx_sample(page_table, qi.reshape(n, IDX_HEADS, IDX_DIM), widx.reshape(n, IDX_HEADS, 1),
                     ki.reshape(n, 1, IDX_DIM), cki_t, layer)
    topk = min(TOPK_MAX, (n_pages * PAGE_SIZE + 1) // 4)
    thr, cut = _topk_sample(sc, topk)
    col = lambda a: a.reshape(n, ATT_HEADS, HEAD_DIM, 1)
    ob = _attn_sample(page_table, col(q), col(k), col(v), sc, thr, cut, ck_t, cv_t, layer)
    x_new = _post(x.reshape(1, n, D_MODEL), acd.reshape(1, n, 768), ob.reshape(1, n, 256).astype(BF16),
                  mod_s.reshape(1, 6, n, D_MODEL), lw, g_final, final).reshape(n, D_MODEL)
    return x_new, (k.reshape(n, 1, ATT_HEADS, HEAD_DIM), v.reshape(n, 1, ATT_HEADS, HEAD_DIM),
                   ki.reshape(n, 1, IDX_DIM), v_n.reshape(n, 1, W_GRP),
                   jnp.swapaxes(pool_new_t, 0, 1), jnp.swapaxes(conv_new_t, 0, 1))


def kernel(x_prompt, x_sample, cache_k, cache_v, cache_kidx, state_pool, state_conv, page_table, c_prompt, c_sample,
           w_mod, b_mod, g_attn_norm, g_mlp_norm, w_in, w_out, g_gmlp, w_spatial, b_spatial, w_pool, s_pool, w_dw, b_dw,
           g_conv_norm, b_conv_norm, w_up, w_down, g_final):
    B, S, _ = x_prompt.shape
    n = x_sample.shape[0]
    depth = w_mod.shape[0]
    past = page_table.shape[1] * PAGE_SIZE
    tabs_p = _tables(jnp.arange(S))
    tabs_s = _tables(jnp.full((1,), past, jnp.int32))
    g_fin = g_final.reshape(1, D_MODEL)
    pad = (-B) % SUBLANES
    c_rows = jnp.concatenate([c_prompt, jnp.zeros((pad, D_MODEL), F32), c_sample], axis=0)
    xp, xs = x_prompt, x_sample.reshape(n, D_MODEL)
    ck_t = jnp.transpose(cache_k, (0, 1, 3, 4, 2))
    cv_t = jnp.transpose(cache_v, (0, 1, 3, 4, 2))
    cki_t = jnp.transpose(cache_kidx, (0, 1, 3, 2))
    outs_p, outs_s = [], []
    for l in range(depth):
        lw = _layer_weights(l, w_mod, b_mod, g_attn_norm, g_mlp_norm, w_in, w_out, g_gmlp, w_spatial, b_spatial,
                            w_pool, s_pool, w_dw, b_dw, g_conv_norm, b_conv_norm, w_up, w_down)
        mod = _modulation(c_rows, lw["w_mod"], lw["b_mod"])
        mod_p = mod[:B].reshape(B, 6, D_MODEL)
        mod_s = jnp.swapaxes(mod[B + pad:].reshape(n, 6, D_MODEL), 0, 1)
        final = l == depth - 1
        xp, op = _prompt_layer(xp, mod_p, lw, tabs_p, g_fin, final)
        xs, os_ = _sample_layer(xs, mod_s, lw, tabs_s, g_fin, final, l, ck_t, cv_t, cki_t,
                                page_table, state_pool[l], state_conv[l])
        outs_p.append(op)
        outs_s.append(os_)
    stack = lambda outs, j: jnp.stack([o[j] for o in outs])
    return (xp, xs.reshape(n, 1, D_MODEL),
            stack(outs_p, 0), stack(outs_p, 1), stack(outs_p, 2), stack(outs_p, 3), stack(outs_p, 4),
            stack(outs_s, 0), stack(outs_s, 1), stack(outs_s, 2), stack(outs_s, 3), stack(outs_s, 4), stack(outs_s, 5))
```

```python
import functools

import jax
import jax.numpy as jnp
import numpy as np
from jax import lax
from jax.experimental import pallas as pl
from jax.experimental.pallas import tpu as pltpu

D_MODEL = 1024
W_GRP = 256
HEAD_DIM = 64
A_HEADS = 4
ATT_HEADS = 4
IDX_HEADS = 8
IDX_DIM = 64
CHUNK = 128
TOPK_MAX = 256
PAGE_SIZE = 128
ROPE_THETA = 10000.0
POOL_WINDOWS = (2, 4, 8, 16)
POOL_STATE = 15
CONV_WIDTH = 31
CONV_STATE = 30
CONV_GROUPS = 4
D_FF = 4 * D_MODEL
EPS = 1e-6
IN_SIZES = (256, 256, 256, 256, 256, 512, 64, 8, 256, 256, 256)
D_IN = sum(IN_SIZES)
KIW_OFF = 1792
CX_OFF_RAW = 1864
D_IN_PAD = 2688
CX_OFF, DA_OFF, DG_OFF = 1920, 2176, 2432
WT_Q, WT_QI, WT_V, WT_W, WT_ROWS = 0, 256, 768, 1024, 1040

LANES = 128
SUBLANES = 8
VMEM_LIMIT = 56 * 1024 * 1024

PRE_TM = 256
POST_TM = 512
FF_CHUNK = 1024
QB = 256
KT = 128
KT_UNROLL = 4
BISECT_ITERS = 32
PACK16 = 16
INT16_MIN = -32768
POOL_HALO = 16
CONV_HALO = 32

F32 = jnp.float32
BF16 = jnp.bfloat16
NEG_INF = float("-inf")
KEY_NEG_INF = -2139095041
KEY_POS_INF = 2139095040


def _cparams(sem):
    return pltpu.CompilerParams(dimension_semantics=sem, vmem_limit_bytes=VMEM_LIMIT)


def _dot(a, b):
    return jnp.dot(a, b, preferred_element_type=F32)


def _dot_nt(a, b):
    return lax.dot_general(a, b, (((1,), (1,)), ((), ())), preferred_element_type=F32)


def _group_mean(x, gmat):
    hi = x.astype(BF16)
    lo = (x - hi.astype(F32)).astype(BF16)
    return _dot(hi, gmat) + _dot(lo, gmat)


def _rot_half(x):
    n = x.shape[-1]
    lane = lax.broadcasted_iota(jnp.int32, x.shape, x.ndim - 1)
    fwd = pltpu.roll(x, n - HEAD_DIM // 2, x.ndim - 1)
    bwd = pltpu.roll(x, HEAD_DIM // 2, x.ndim - 1)
    return jnp.where((lane % HEAD_DIM) < HEAD_DIM // 2, fwd, bwd)


def _rope(x, cos, sin_signed):
    reps = x.shape[-1] // LANES
    if reps > 1:
        cos = jnp.concatenate([cos] * reps, axis=-1)
        sin_signed = jnp.concatenate([sin_signed] * reps, axis=-1)
    return x * cos + _rot_half(x) * sin_signed


def _sigmoid(x):
    return 1.0 / (1.0 + jnp.exp(-x))


def _mod_kernel(c_ref, w_ref, b_ref, o_ref):
    c = c_ref[...]
    a = (c * _sigmoid(c)).astype(BF16)
    o_ref[...] = _dot(a, w_ref[...].astype(BF16)) + b_ref[...]


def _modulation(c_rows, w_mod_l, b_mod_l):
    n = c_rows.shape[0]
    ncol = w_mod_l.shape[1]
    tn = 1024
    return pl.pallas_call(
        _mod_kernel,
        out_shape=jax.ShapeDtypeStruct((n, ncol), F32),
        grid=(ncol // tn,),
        in_specs=[
            pl.BlockSpec((n, D_MODEL), lambda j: (0, 0)),
            pl.BlockSpec((D_MODEL, tn), lambda j: (0, j)),
            pl.BlockSpec((1, tn), lambda j: (0, j)),
        ],
        out_specs=pl.BlockSpec((n, tn), lambda j: (0, j)),
        compiler_params=_cparams(("arbitrary",)),
        name="adaln_mod",
    )(c_rows, w_mod_l, b_mod_l.reshape(1, ncol))


def _modulated_norm(x, g, shift, scale):
    ms = jnp.mean(x * x, axis=-1, keepdims=True)
    y = x * lax.rsqrt(ms + EPS) * g
    return y * (1.0 + scale) + shift


def _head_rmsnorm(av, g_row, gmat):
    ms = _group_mean(av * av, gmat)
    return av * lax.rsqrt(ms + EPS) * g_row


def _group_layernorm_silu(y, g_row, b_row, gmat):
    mu = _group_mean(y, gmat)
    d = y - mu
    var = _group_mean(d * d, gmat)
    yn = d * lax.rsqrt(var + EPS) * g_row + b_row
    return yn * _sigmoid(yn)


def _rope_t(x, cos_t, sin_t):
    half = HEAD_DIM // 2
    out = []
    for hh in range(x.shape[0] // HEAD_DIM):
        a = x[hh * HEAD_DIM:hh * HEAD_DIM + half]
        b = x[hh * HEAD_DIM + half:(hh + 1) * HEAD_DIM]
        out.append(a * cos_t - b * sin_t)
        out.append(b * cos_t + a * sin_t)
    return jnp.concatenate(out, axis=0)


def _pre_prompt_kernel(x_ref, mod_ref, gattn_ref, win_ref, wint_ref, ggmlp_ref, wsp_ref, bsp_ref, cos_ref, sin_ref,
                       cost_ref, sint_ref, wpool_ref, spool_ref, wdw_ref, bdw_ref, gcn_ref, bcn_ref, gmat_ref,
                       acd_ref, qt_ref, qit_ref, wt_ref, kbf_ref, kibf_ref, vt_ref,
                       k_ref, v_ref, ki_ref, poolst_ref, convst_ref,
                       pool_buf, conv_buf):
    i = pl.program_id(1)
    tm = x_ref.shape[1]
    gmat = gmat_ref[...]

    @pl.when(i == 0)
    def _():
        pool_buf[0:POOL_HALO, :] = jnp.zeros((POOL_HALO, W_GRP), F32)
        conv_buf[0:CONV_HALO, :] = jnp.zeros((CONV_HALO, W_GRP), F32)

    mod = mod_ref[0]
    h = _modulated_norm(x_ref[0], gattn_ref[...], mod[0:1], mod[1:2]).astype(BF16)

    def proj(off, width):
        return _dot(h, win_ref[:, off:off + width])

    a_u = proj(0, 256)
    v_n = _head_rmsnorm(proj(256, 256), ggmlp_ref[...], gmat)
    v_bf = v_n.astype(BF16)
    lane256 = lax.broadcasted_iota(jnp.int32, (CHUNK, 256), 1)
    rr = lax.broadcasted_iota(jnp.int32, (CHUNK, CHUNK), 0)
    cc = lax.broadcasted_iota(jnp.int32, (CHUNK, CHUNK), 1)
    ws = [jnp.where(cc <= rr, wsp_ref[hh], 0.0).astype(BF16) for hh in range(A_HEADS)]
    gate_chunks = []
    for c in range(tm // CHUNK):
        vc = v_bf[c * CHUNK:(c + 1) * CHUNK, :]
        s = [_dot(ws[hh], vc) for hh in range(A_HEADS)]
        hid = lane256 // HEAD_DIM
        g = jnp.where(hid == 0, s[0], jnp.where(hid == 1, s[1], jnp.where(hid == 2, s[2], s[3])))
        gate_chunks.append(g + bsp_ref[...])
    out_a = a_u * jnp.concatenate(gate_chunks, axis=0)

    cos = cos_ref[...]
    sin = sin_ref[...]
    k = _rope(proj(768, 256), cos, sin)
    k_ref[0] = k
    kbf_ref[0] = k.astype(BF16)
    v_ref[0] = proj(1024, 256)
    ki = _rope(proj(KIW_OFF, LANES), cos, sin)[:, 0:IDX_DIM]
    ki_ref[0] = ki
    kibf_ref[0] = ki.astype(BF16)

    def proj_t(off, width):
        return _dot_nt(wint_ref[off:off + width, :], h)

    cos_t = cost_ref[...]
    sin_t = sint_ref[...]
    qt_ref[0] = (_rope_t(proj_t(WT_Q, 256), cos_t, sin_t) * (HEAD_DIM ** -0.5)).astype(BF16)
    qit_ref[0] = _rope_t(proj_t(WT_QI, 512), cos_t, sin_t).astype(BF16)
    v_t = proj_t(WT_V, 256).astype(BF16)
    for c in range(tm // KT):
        vt_ref[0, c] = v_t[:, c * KT:(c + 1) * KT]
    wt_ref[0] = proj_t(WT_W, 16)[0:IDX_HEADS, :] * ((IDX_HEADS * IDX_DIM) ** -0.5)

    c_x = proj(CX_OFF, 256)
    pool_buf[POOL_HALO:POOL_HALO + tm, :] = c_x
    lane128 = lax.broadcasted_iota(jnp.int32, (tm, LANES), 1)
    tpos = i * tm + lax.broadcasted_iota(jnp.int32, (tm, LANES), 0)
    upper = lane128 >= HEAD_DIM
    pooled = []
    for col, (w_lo, w_hi) in enumerate(((2, 4), (8, 16))):
        cs = slice(col * LANES, (col + 1) * LANES)
        s_lo = pool_buf[POOL_HALO:POOL_HALO + tm, cs]
        for j in range(1, w_lo):
            s_lo = s_lo + pool_buf[POOL_HALO - j:POOL_HALO - j + tm, cs]
        s_hi = pool_buf[POOL_HALO - w_lo:POOL_HALO - w_lo + tm, cs]
        for j in range(w_lo + 1, w_hi):
            s_hi = s_hi + pool_buf[POOL_HALO - j:POOL_HALO - j + tm, cs]
        win = jnp.where(upper, w_hi, w_lo)
        cnt = jnp.minimum(win, tpos + 1).astype(F32)
        pooled.append(jnp.where(upper, s_lo + s_hi, s_lo) / cnt)
    diff = (jnp.concatenate(pooled, axis=-1) - c_x).astype(BF16)
    out_c = _dot(diff, wpool_ref[...]) * spool_ref[...]
    poolst_ref[0] = pool_buf[tm:tm + POOL_HALO, :]
    pool_buf[0:POOL_HALO, :] = pool_buf[tm:tm + POOL_HALO, :]

    glu = proj(DA_OFF, 256) * _sigmoid(proj(DG_OFF, 256))
    conv_buf[CONV_HALO:CONV_HALO + tm, :] = glu
    base = CONV_HALO - CONV_STATE
    y = conv_buf[base:base + tm, :] * wdw_ref[0:1, :]
    for j in range(1, CONV_WIDTH):
        y = y + conv_buf[base + j:base + j + tm, :] * wdw_ref[j:j + 1, :]
    y = y + bdw_ref[...]
    out_d = _group_layernorm_silu(y, gcn_ref[...], bcn_ref[...], gmat)
    convst_ref[0] = conv_buf[tm:tm + CONV_HALO, :]
    conv_buf[0:CONV_HALO, :] = conv_buf[tm:tm + CONV_HALO, :]

    acd_ref[0] = jnp.concatenate([out_a, out_c, out_d], axis=-1).astype(BF16)


def _pre_prompt(x, mod, lw, tabs):
    B, S, _ = x.shape
    tm = min(PRE_TM, S)
    n_t = S // tm
    tok = lambda w: pl.BlockSpec((1, tm, w), lambda b, i: (b, i, 0))
    const = lambda shape: pl.BlockSpec(shape, lambda b, i: tuple(0 for _ in shape))
    in_specs = [
        tok(D_MODEL),
        pl.BlockSpec((1, 6, D_MODEL), lambda b, i: (b, 0, 0)),
        const((1, D_MODEL)),
        const((D_MODEL, D_IN_PAD)),
        const((WT_ROWS, D_MODEL)),
        const((1, W_GRP)),
        const((A_HEADS, CHUNK, CHUNK)),
        const((CHUNK, W_GRP)),
        pl.BlockSpec((tm, LANES), lambda b, i: (i, 0)),
        pl.BlockSpec((tm, LANES), lambda b, i: (i, 0)),
        pl.BlockSpec((HEAD_DIM // 2, tm), lambda b, i: (0, i)),
        pl.BlockSpec((HEAD_DIM // 2, tm), lambda b, i: (0, i)),
        const((W_GRP, W_GRP)),
        const((1, W_GRP)),
        const((32, W_GRP)),
        const((1, W_GRP)),
        const((1, W_GRP)),
        const((1, W_GRP)),
        const((W_GRP, W_GRP)),
    ]
    out_shapes = [
        jax.ShapeDtypeStruct((B, S, 768), BF16),
        jax.ShapeDtypeStruct((B, 256, S), BF16),
        jax.ShapeDtypeStruct((B, 512, S), BF16),
        jax.ShapeDtypeStruct((B, IDX_HEADS, S), F32),
        jax.ShapeDtypeStruct((B, S, 256), BF16),
        jax.ShapeDtypeStruct((B, S, IDX_DIM), BF16),
        jax.ShapeDtypeStruct((B, S // KT, 256, KT), BF16),
        jax.ShapeDtypeStruct((B, S, 256), F32),
        jax.ShapeDtypeStruct((B, S, 256), F32),
        jax.ShapeDtypeStruct((B, S, IDX_DIM), F32),
        jax.ShapeDtypeStruct((B, POOL_HALO, W_GRP), F32),
        jax.ShapeDtypeStruct((B, CONV_HALO, W_GRP), F32),
    ]
    feat = lambda w: pl.BlockSpec((1, w, tm), lambda b, i: (b, 0, i))
    out_specs = [tok(768), feat(256), feat(512), feat(IDX_HEADS), tok(256), tok(IDX_DIM),
                 pl.BlockSpec((1, tm // KT, 256, KT), lambda b, i: (b, i, 0, 0)),
                 tok(256), tok(256), tok(IDX_DIM),
                 pl.BlockSpec((1, POOL_HALO, W_GRP), lambda b, i: (b, 0, 0)),
                 pl.BlockSpec((1, CONV_HALO, W_GRP), lambda b, i: (b, 0, 0))]
    return pl.pallas_call(
        _pre_prompt_kernel,
        out_shape=out_shapes,
        grid=(B, n_t),
        in_specs=in_specs,
        out_specs=out_specs,
        scratch_shapes=[pltpu.VMEM((POOL_HALO + tm, W_GRP), F32), pltpu.VMEM((CONV_HALO + tm, W_GRP), F32)],
        compiler_params=_cparams(("arbitrary", "arbitrary")),
        name="pre_prompt",
    )(x, mod, lw["g_attn"], lw["w_in"], lw["w_in_t"], lw["g_gmlp"], lw["w_spatial"], lw["b_spatial_t"],
      tabs["cos"], tabs["sin"], tabs["cos_t"], tabs["sin_t"],
      lw["w_pool_bd"], lw["s_pool"], lw["w_dw"], lw["b_dw"], lw["g_cn"], lw["b_cn"], tabs["gmat"])


def _key_to_float(key):
    bits = jnp.where(key >= 0, key, key ^ jnp.int32(0x7FFFFFFF))
    return pltpu.bitcast(bits, F32)


def _float_to_key(x):
    bits = pltpu.bitcast(x, jnp.int32)
    return jnp.where(bits >= 0, bits, bits ^ jnp.int32(0x7FFFFFFF))


def _dsa_prompt_kernel(qt_ref, qit_ref, wt_ref, ki_ref, k_ref, vt_ref, o_ref, sc_ref, lg_ref, h16_ref, l16_ref,
                       *, topk):
    i = pl.program_id(1)
    qb = qt_ref.shape[2]
    blk_tiles = qb // KT
    n_real = (i + 1) * blk_tiles
    n_it = (n_real + KT_UNROLL - 1) // KT_UNROLL
    krow = lax.broadcasted_iota(jnp.int32, (KT, qb), 0)
    q_pos = i * qb + lax.broadcasted_iota(jnp.int32, (KT, qb), 1)
    q_pos1 = i * qb + lax.broadcasted_iota(jnp.int32, (1, qb), 1)

    def fold_sum(x):
        return jnp.sum(x.reshape(KT // SUBLANES, SUBLANES, qb), axis=0)

    def fold_max(x):
        return jnp.max(x.reshape(KT // SUBLANES, SUBLANES, qb), axis=0)

    def fold16(x):
        parts = [x[j * PACK16:(j + 1) * PACK16] for j in range(KT // PACK16)]
        while len(parts) > 1:
            parts = [a + b for a, b in zip(parts[0::2], parts[1::2])]
        return parts[0]

    def tile_of(it, u):
        return it * KT_UNROLL + u

    def key_rows(kt, n=KT):
        return pl.ds(pl.multiple_of(kt * KT, KT), n)

    w_rows = [wt_ref[0, hh:hh + 1, :] for hh in range(IDX_HEADS)]

    def score_tiles(kt0, diagonal):
        kc = ki_ref[0, key_rows(kt0, blk_tiles * KT), :]
        acc_all = jnp.zeros((blk_tiles * KT, qb), F32)
        for hh in range(IDX_HEADS):
            d = _dot(kc, qit_ref[0, hh * IDX_DIM:(hh + 1) * IDX_DIM, :])
            acc_all = acc_all + w_rows[hh] * jnp.maximum(d, 0.0)
        for u in range(blk_tiles):
            kt = kt0 + u
            acc = acc_all[u * KT:(u + 1) * KT]
            hi16 = _float_to_key(acc) >> 16
            if diagonal:
                causal = kt * KT + krow <= q_pos
                acc = jnp.where(causal, acc, NEG_INF)
                hi16 = jnp.where(causal, hi16, INT16_MIN)
            sc_ref[kt] = acc
            h16_ref[kt] = hi16.astype(jnp.int16)

    def score_body(it, carry):
        score_tiles(it * blk_tiles, False)
        return carry

    lax.fori_loop(0, i, score_body, 0)
    score_tiles(i * blk_tiles, True)

    @pl.when(n_it * KT_UNROLL > n_real)
    def _():
        for u in range(blk_tiles):
            sc_ref[n_real + u] = jnp.full((KT, qb), NEG_INF, F32)
            h16_ref[n_real + u] = jnp.full((KT, qb), INT16_MIN, jnp.int16)

    def count16(ref, t):
        t16 = t.astype(jnp.int16)

        def body(it, acc):
            for u in range(KT_UNROLL):
                hit = jnp.where(ref[tile_of(it, u)] >= t16, jnp.int16(1), jnp.int16(0))
                acc = acc + fold16(hit)
            return acc
        acc = lax.fori_loop(0, n_it, body, jnp.zeros((PACK16, qb), jnp.int16))
        return jnp.sum(acc.astype(jnp.int32), axis=0, keepdims=True)

    def count_where(pred):
        def body(it, acc):
            for u in range(KT_UNROLL):
                kt = tile_of(it, u)
                acc = acc + fold_sum(jnp.where(pred(sc_ref[kt], kt), 1.0, 0.0))
            return acc
        acc = lax.fori_loop(0, n_it, body, jnp.zeros((SUBLANES, qb), F32))
        return jnp.sum(acc, axis=0, keepdims=True).astype(jnp.int32)

    k_row = jnp.minimum(topk, q_pos1 + 1)

    def upper_body(_, carry):
        lo, hi, cnt_hi = carry
        mid = (lo + hi) >> 1
        cnt = count16(h16_ref, mid)
        ge = cnt >= k_row
        return jnp.where(ge, mid, lo), jnp.where(ge, hi, mid), jnp.where(ge, cnt_hi, cnt)

    full = lambda v: jnp.full((1, qb), v, jnp.int32)
    t_hi, _, n_above = lax.fori_loop(0, 16, upper_body, (full(INT16_MIN + 1), full(-INT16_MIN), full(0)))

    def lower_prep(it, carry):
        for u in range(KT_UNROLL):
            kt = tile_of(it, u)
            s = sc_ref[kt]
            key = _float_to_key(s)
            low = jnp.where((key >> 16) == t_hi, (key & 0xFFFF) >> 1, -1)
            l16_ref[kt] = jnp.where(s == NEG_INF, -1, low).astype(jnp.int16)
        return carry

    lax.fori_loop(0, n_it, lower_prep, 0)
    k_low = k_row - n_above

    def lower_body(_, carry):
        lo, hi = carry
        mid = (lo + hi) >> 1
        ge = count16(l16_ref, mid) >= k_low
        return jnp.where(ge, mid, lo), jnp.where(ge, hi, mid)

    t_lo, _ = lax.fori_loop(0, 15, lower_body, (full(0), full(-INT16_MIN)))
    key_even = (t_hi << 16) | (t_lo << 1)
    count_ge_key = lambda key: count_where(lambda s, kt: s >= _key_to_float(key))
    key_thr = jnp.where(count_ge_key(key_even + 1) >= k_row, key_even + 1, key_even)
    thr_packed = _key_to_float(key_thr)

    def ge_gt_body(it, carry):
        ge, gt = carry
        for u in range(KT_UNROLL):
            s = sc_ref[tile_of(it, u)]
            ge = ge + fold_sum(jnp.where(s >= thr_packed, 1.0, 0.0))
            gt = gt + fold_sum(jnp.where(s > thr_packed, 1.0, 0.0))
        return ge, gt

    zero8 = jnp.zeros((SUBLANES, qb), F32)
    ge8, gt8 = lax.fori_loop(0, n_it, ge_gt_body, (zero8, zero8))
    n_ge = jnp.sum(ge8, axis=0, keepdims=True).astype(jnp.int32)
    n_gt = jnp.sum(gt8, axis=0, keepdims=True).astype(jnp.int32)
    consistent = (n_ge >= k_row) & (n_gt < k_row)

    def compare_bisect(_):
        def body(_, carry):
            lo, hi = carry
            mid = (lo >> 1) + (hi >> 1) + (lo & hi & 1)
            ge = count_ge_key(mid) >= k_row
            return jnp.where(ge, mid, lo), jnp.where(ge, hi, mid)
        lo, _ = lax.fori_loop(0, BISECT_ITERS, body, (full(KEY_NEG_INF + 1), full(KEY_POS_INF + 1)))
        return lo, count_ge_key(lo)

    key_thr, n_ge = lax.cond(jnp.max(jnp.where(consistent, 0.0, 1.0)) > 0.0, compare_bisect,
                             lambda _: (key_thr, n_ge), 0)
    thr = _key_to_float(key_thr)

    def selected(s, kt, cut):
        return (s > thr) | ((s == thr) & (kt * KT + krow <= cut))

    n_keys = n_it * KT_UNROLL * KT
    any_excess = jnp.max(jnp.where(n_ge > k_row, 1.0, 0.0)) > 0.0

    def tie_cut(_):
        def body(_, carry):
            lo_c, hi_c = carry
            mid = (lo_c + hi_c) >> 1
            ge = count_where(lambda s, kt: selected(s, kt, mid)) >= k_row
            return jnp.where(ge, lo_c, mid), jnp.where(ge, mid, hi_c)
        lo_c = jnp.full((1, qb), -1, jnp.int32)
        hi_c = jnp.full((1, qb), 1, jnp.int32) * (n_keys - 1)
        _, hi_c = lax.fori_loop(0, 13, body, (lo_c, hi_c))
        return hi_c

    cut = lax.cond(any_excess, tie_cut, lambda _: jnp.full((1, qb), 1, jnp.int32) * n_keys, 0)

    def bias_body(it, carry):
        for u in range(KT_UNROLL):
            kt = tile_of(it, u)
            sc_ref[kt] = jnp.where(selected(sc_ref[kt], kt, cut), 0.0, NEG_INF)
        return carry

    lax.fori_loop(0, n_it, bias_body, 0)

    pair_row = lax.broadcasted_iota(jnp.int32, (2 * HEAD_DIM, qb), 0)
    n_pairs = ATT_HEADS // 2
    q2s = []
    for p in range(n_pairs):
        slab = qt_ref[0, p * LANES:(p + 1) * LANES, :].astype(F32)
        q2s.append(jnp.concatenate([jnp.where(pair_row < HEAD_DIM, slab, 0.0),
                                    jnp.where(pair_row >= HEAD_DIM, slab, 0.0)], axis=1).astype(BF16))

    def logit_step(p, it, mruns):
        mruns = list(mruns)
        lg4 = _dot(k_ref[0, key_rows(it * KT_UNROLL, KT_UNROLL * KT), p * LANES:(p + 1) * LANES], q2s[p])
        for u in range(KT_UNROLL):
            kt = tile_of(it, u)
            for half in range(2):
                lgt = lg4[u * KT:(u + 1) * KT, half * qb:(half + 1) * qb] + sc_ref[kt]
                lg_ref[2 * p + half, kt] = lgt
                mruns[half] = jnp.maximum(mruns[half], fold_max(lgt))
        return tuple(mruns)

    def pv_step(p, mx, it, carry):
        lruns, accs = list(carry[0]), list(carry[1])
        for half in range(2):
            hh = 2 * p + half
            pes = []
            for u in range(KT_UNROLL):
                pe = jnp.exp(lg_ref[hh, tile_of(it, u)] - mx[half])
                lruns[half] = lruns[half] + fold_sum(pe)
                pes.append(pe.astype(BF16))
            v4 = jnp.concatenate([vt_ref[0, tile_of(it, u), hh * HEAD_DIM:(hh + 1) * HEAD_DIM, :]
                                  for u in range(KT_UNROLL)], axis=1)
            accs[half] = accs[half] + _dot(v4, jnp.concatenate(pes, axis=0))
        return tuple(lruns), tuple(accs)

    neg = jnp.full((SUBLANES, qb), NEG_INF, F32)
    zl = jnp.zeros((SUBLANES, qb), F32)
    za = jnp.zeros((HEAD_DIM, qb), F32)
    row_max = lambda mruns: [jnp.max(m, axis=0, keepdims=True) for m in mruns]
    pv_init = ((zl, zl), (za, za))

    mx0 = row_max(lax.fori_loop(0, n_it, lambda it, m: logit_step(0, it, m), (neg, neg)))
    mruns1, pv0 = lax.fori_loop(
        0, n_it, lambda it, c: (logit_step(1, it, c[0]), pv_step(0, mx0, it, c[1])), ((neg, neg), pv_init))
    mx1 = row_max(mruns1)
    pv1 = lax.fori_loop(0, n_it, lambda it, c: pv_step(1, mx1, it, c), pv_init)

    o_heads = []
    for lruns, accs in (pv0, pv1):
        for half in range(2):
            o_heads.append(accs[half] / jnp.sum(lruns[half], axis=0, keepdims=True))
    o_ref[0] = jnp.concatenate(o_heads, axis=0).T.astype(BF16)


def _dsa_prompt(qt, qit, wt, kibf, kbf, vt):
    B, _, S = qt.shape
    qb = min(QB, S)
    n_kt = S // KT
    assert S % (KT * KT_UNROLL) == 0 and qb % KT == 0
    feat = lambda w: pl.BlockSpec((1, w, qb), lambda b, i: (b, 0, i))
    full = lambda w: pl.BlockSpec((1, S, w), lambda b, i: (b, 0, 0))
    return pl.pallas_call(
        functools.partial(_dsa_prompt_kernel, topk=min(TOPK_MAX, S // 4)),
        out_shape=jax.ShapeDtypeStruct((B, S, 256), BF16),
        grid=(B, S // qb),
        in_specs=[feat(256), feat(512), feat(IDX_HEADS), full(IDX_DIM), full(256),
                  pl.BlockSpec((1, n_kt, 256, KT), lambda b, i: (b, 0, 0, 0))],
        out_specs=pl.BlockSpec((1, qb, 256), lambda b, i: (b, i, 0)),
        scratch_shapes=[pltpu.VMEM((n_kt, KT, qb), F32), pltpu.VMEM((ATT_HEADS, n_kt, KT, qb), F32),
                        pltpu.VMEM((n_kt, KT, qb), jnp.int16), pltpu.VMEM((n_kt, KT, qb), jnp.int16)],
        compiler_params=_cparams(("arbitrary", "arbitrary")),
        name="dsa_prompt",
    )(qt, qit, wt, kibf, kbf, vt)


def _post_kernel(x_ref, acd_ref, ob_ref, mod_ref, gmlp_ref, woacd_ref, wob_ref, wup_ref, wdn_ref, gfin_ref, o_ref,
                 *, final):
    mod = mod_ref[0]
    mixed = _dot(acd_ref[0], woacd_ref[...]) + _dot(ob_ref[0], wob_ref[...])
    x1 = x_ref[0] + mod[2] * mixed
    h2 = _modulated_norm(x1, gmlp_ref[...], mod[3], mod[4]).astype(BF16)
    acc = jnp.zeros_like(x1)
    for c in range(D_FF // FF_CHUNK):
        u = jnp.maximum(_dot(h2, wup_ref[:, c * FF_CHUNK:(c + 1) * FF_CHUNK]), 0.0)
        acc = acc + _dot((u * u).astype(BF16), wdn_ref[c * FF_CHUNK:(c + 1) * FF_CHUNK, :])
    x2 = x1 + mod[5] * acc
    if final:
        ms = jnp.mean(x2 * x2, axis=-1, keepdims=True)
        x2 = x2 * lax.rsqrt(ms + EPS) * gfin_ref[...]
    o_ref[0] = x2


def _post(x, acd, ob, mod, lw, g_final, final):
    G, T, _ = x.shape
    tm = min(POST_TM, T)
    R = mod.shape[2]
    tok = lambda w: pl.BlockSpec((1, tm, w), lambda b, i: (b, i, 0))
    const = lambda shape: pl.BlockSpec(shape, lambda b, i: tuple(0 for _ in shape), pipeline_mode=pl.Buffered(1))
    if R == 1:
        mod_spec = pl.BlockSpec((1, 6, 1, D_MODEL), lambda b, i: (b, 0, 0, 0))
    else:
        mod_spec = pl.BlockSpec((1, 6, tm, D_MODEL), lambda b, i: (b, 0, i, 0))
    return pl.pallas_call(
        functools.partial(_post_kernel, final=final),
        out_shape=jax.ShapeDtypeStruct((G, T, D_MODEL), F32),
        grid=(G, T // tm),
        in_specs=[tok(D_MODEL), tok(768), tok(256), mod_spec, const((1, D_MODEL)),
                  const((768, D_MODEL)), const((256, D_MODEL)), const((D_MODEL, D_FF)), const((D_FF, D_MODEL)),
                  const((1, D_MODEL))],
        out_specs=tok(D_MODEL),
        compiler_params=_cparams(("arbitrary", "arbitrary")),
        name="post_final" if final else "post",
    )(x, acd, ob, mod, lw["g_mlp"], lw["w_out_acd"], lw["w_out_b"], lw["w_up"], lw["w_down"], g_final)


def _layer_weights(l, w_mod, b_mod, g_attn_norm, g_mlp_norm, w_in, w_out, g_gmlp, w_spatial, b_spatial,
                   w_pool, s_pool, w_dw, b_dw, g_conv_norm, b_conv_norm, w_up, w_down):
    w_in_l = w_in[l]
    w_in_pad = jnp.concatenate(
        [w_in_l[:, :CX_OFF_RAW], jnp.zeros((D_MODEL, CX_OFF - CX_OFF_RAW), w_in_l.dtype), w_in_l[:, CX_OFF_RAW:]],
        axis=1).astype(BF16)
    w_pool_bd = jnp.zeros((W_GRP, W_GRP), F32)
    for g in range(len(POOL_WINDOWS)):
        w_pool_bd = w_pool_bd.at[g * 64:(g + 1) * 64, g * 64:(g + 1) * 64].set(w_pool[l, g])
    w_out_l = w_out[l]
    w_in_t = jnp.concatenate(
        [w_in_l[:, 512:768], w_in_l[:, 1280:1792], w_in_l[:, 1024:1280], w_in_l[:, 1856:1864],
         jnp.zeros((D_MODEL, WT_ROWS - WT_W - IDX_HEADS), w_in_l.dtype)], axis=1).T.astype(BF16)
    return dict(
        w_in_t=w_in_t,
        w_mod=w_mod[l], b_mod=b_mod[l],
        g_attn=g_attn_norm[l].reshape(1, D_MODEL), g_mlp=g_mlp_norm[l].reshape(1, D_MODEL),
        w_in=w_in_pad,
        g_gmlp=g_gmlp[l].reshape(1, W_GRP),
        w_spatial=w_spatial[l],
        b_spatial_t=jnp.repeat(b_spatial[l].T, HEAD_DIM, axis=1),
        w_sp0=jnp.repeat(w_spatial[l, :, 0, 0], HEAD_DIM).reshape(1, W_GRP),
        b_sp0=jnp.repeat(b_spatial[l, :, 0], HEAD_DIM).reshape(1, W_GRP),
        w_pool_bd=w_pool_bd.astype(BF16), s_pool=s_pool[l].reshape(1, W_GRP),
        w_dw=jnp.concatenate([w_dw[l], jnp.zeros((1, W_GRP), F32)], axis=0), b_dw=b_dw[l].reshape(1, W_GRP),
        g_cn=g_conv_norm[l].reshape(1, W_GRP), b_cn=b_conv_norm[l].reshape(1, W_GRP),
        w_out_acd=jnp.concatenate([w_out_l[0:256], w_out_l[512:1024]], axis=0).astype(BF16),
        w_out_b=w_out_l[256:512].astype(BF16),
        w_up=w_up[l].astype(BF16), w_down=w_down[l].astype(BF16),
    )


def _tables(pos):
    half = HEAD_DIM // 2
    freqs = ROPE_THETA ** (-jnp.arange(half, dtype=F32) / half)
    ang = pos.astype(F32)[:, None] * freqs[None, :]
    cos = jnp.tile(jnp.cos(ang), (1, LANES // half))
    sin = jnp.sin(ang)
    sin_signed = jnp.tile(jnp.concatenate([-sin, sin], axis=1), (1, LANES // HEAD_DIM))
    gid = jnp.arange(W_GRP) // HEAD_DIM
    gmat = jnp.where(gid[:, None] == gid[None, :], 1.0 / HEAD_DIM, 0.0).astype(BF16)
    return dict(cos=cos, sin=sin_signed, gmat=gmat, cos_t=jnp.cos(ang).T, sin_t=sin.T)


def _prompt_layer(x, mod_p, lw, tabs, g_final, final):
    B, S, _ = x.shape
    (acd, qt, qit, wt, kbf, kibf, vt, k, v, ki, poolst, convst) = _pre_prompt(x, mod_p, lw, tabs)
    ob = _dsa_prompt(qt, qit, wt, kibf, kbf, vt)
    x_new = _post(x, acd, ob, mod_p.reshape(B, 6, 1, D_MODEL), lw, g_final, final)
    return x_new, (k.reshape(B, S, ATT_HEADS, HEAD_DIM), v.reshape(B, S, ATT_HEADS, HEAD_DIM), ki,
                   poolst[:, POOL_HALO - POOL_STATE:], convst[:, CONV_HALO - CONV_STATE:])


def _pre_sample_kernel(x_ref, mod_ref, gattn_ref, win_ref, ggmlp_ref, wsp0_ref, bsp0_ref, cos_ref, sin_ref,
                       wpool_ref, spool_ref, wdw_ref, bdw_ref, gcn_ref, bcn_ref, gmat_ref, poolprev_ref, convprev_ref,
                       acd_ref, q_ref, qi_ref, widx_ref, k_ref, v_ref, ki_ref, vn_ref, poolnew_ref, convnew_ref):
    gmat = gmat_ref[...]
    h = _modulated_norm(x_ref[...], gattn_ref[...], mod_ref[0], mod_ref[1]).astype(BF16)

    def proj(off, width):
        return _dot(h, win_ref[:, off:off + width])

    a_u = proj(0, 256)
    v_n = _head_rmsnorm(proj(256, 256), ggmlp_ref[...], gmat)
    vn_ref[...] = v_n
    out_a = a_u * (wsp0_ref[...] * v_n + bsp0_ref[...])

    cos = cos_ref[...]
    sin = sin_ref[...]
    q_ref[...] = _rope(proj(512, 256), cos, sin) * (HEAD_DIM ** -0.5)
    k_ref[...] = _rope(proj(768, 256), cos, sin)
    v_ref[...] = proj(1024, 256)
    qi_ref[...] = _rope(proj(1280, 512), cos, sin).astype(BF16)
    kiw = proj(KIW_OFF, LANES)
    ki_ref[...] = _rope(kiw, cos, sin)[:, 0:IDX_DIM]
    widx_ref[...] = kiw[:, IDX_DIM:IDX_DIM + IDX_HEADS] * ((IDX_HEADS * IDX_DIM) ** -0.5)

    c_x = proj(CX_OFF, 256)
    sums = {}
    run = c_x
    for j in range(1, max(POOL_WINDOWS)):
        run = run + poolprev_ref[POOL_STATE - j]
        if j + 1 in POOL_WINDOWS:
            sums[j + 1] = run * (1.0 / (j + 1))
    gid = lax.broadcasted_iota(jnp.int32, c_x.shape, 1) // HEAD_DIM
    pooled = jnp.where(gid == 0, sums[2], jnp.where(gid == 1, sums[4], jnp.where(gid == 2, sums[8], sums[16])))
    diff = (pooled - c_x).astype(BF16)
    out_c = _dot(diff, wpool_ref[...]) * spool_ref[...]
    for r in range(POOL_STATE - 1):
        poolnew_ref[r] = poolprev_ref[r + 1]
    poolnew_ref[POOL_STATE - 1] = c_x

    glu = proj(DA_OFF, 256) * _sigmoid(proj(DG_OFF, 256))
    y = glu * wdw_ref[CONV_STATE:CONV_STATE + 1, :]
    for j in range(CONV_STATE):
        y = y + convprev_ref[j] * wdw_ref[j:j + 1, :]
    y = y + bdw_ref[...]
    out_d = _group_layernorm_silu(y, gcn_ref[...], bcn_ref[...], gmat)
    for r in range(CONV_STATE - 1):
        convnew_ref[r] = convprev_ref[r + 1]
    convnew_ref[CONV_STATE - 1] = glu

    acd_ref[...] = jnp.concatenate([out_a, out_c, out_d], axis=-1).astype(BF16)


def _pre_sample(x, mod_s, lw, tabs_s, pool_prev_t, conv_prev_t):
    n = x.shape[0]
    out_shapes = [
        jax.ShapeDtypeStruct((n, 768), BF16),
        jax.ShapeDtypeStruct((n, 256), F32),
        jax.ShapeDtypeStruct((n, 512), BF16),
        jax.ShapeDtypeStruct((n, IDX_HEADS), F32),
        jax.ShapeDtypeStruct((n, 256), F32),
        jax.ShapeDtypeStruct((n, 256), F32),
        jax.ShapeDtypeStruct((n, IDX_DIM), F32),
        jax.ShapeDtypeStruct((n, 256), F32),
        jax.ShapeDtypeStruct((POOL_STATE, n, W_GRP), F32),
        jax.ShapeDtypeStruct((CONV_STATE, n, W_GRP), F32),
    ]
    return pl.pallas_call(
        _pre_sample_kernel,
        out_shape=out_shapes,
        compiler_params=pltpu.CompilerParams(vmem_limit_bytes=VMEM_LIMIT),
        name="pre_sample",
    )(x, mod_s, lw["g_attn"], lw["w_in"], lw["g_gmlp"], lw["w_sp0"], lw["b_sp0"], tabs_s["cos"], tabs_s["sin"],
      lw["w_pool_bd"], lw["s_pool"], lw["w_dw"], lw["b_dw"], lw["g_cn"], lw["b_cn"], tabs_s["gmat"],
      pool_prev_t, conv_prev_t)


SC_CHUNKS = 17
PAGES_PER_CHUNK = 8
NEW_CHUNK = 16
IDX_CHUNK_UNROLL = 4
ATT_PAGES = 16
ATT_SLOTS = 3


def _idx_sample_kernel(pt_ref, qi_ref, w_ref, kinew_ref, cki_ref, sc_ref, kbuf, sems, *, layer):
    b = pl.program_id(0)
    nb = pl.num_programs(0)
    n_pages = kbuf.shape[1]

    def page_copy(bb, p, slot):
        return pltpu.make_async_copy(cki_ref.at[layer, pt_ref[bb, p]], kbuf.at[slot, p], sems.at[slot])

    def start_all(bb, slot):
        def body(p, c):
            page_copy(bb, p, slot).start()
            return c
        lax.fori_loop(0, n_pages, body, 0)

    slot = b % 2

    @pl.when(b == 0)
    def _():
        start_all(0, 0)

    @pl.when(b + 1 < nb)
    def _():
        start_all(b + 1, 1 - slot)

    def wait_body(p, c):
        page_copy(b, p, slot).wait()
        return c
    lax.fori_loop(0, n_pages, wait_body, 0)

    qi = qi_ref[0]
    w = w_ref[0]

    def chunk_body(it, carry):
        for cc in range(IDX_CHUNK_UNROLL):
            c = it * IDX_CHUNK_UNROLL + cc
            tiles = [kbuf[slot, c * PAGES_PER_CHUNK + j].astype(BF16) for j in range(PAGES_PER_CHUNK)]
            d = jnp.maximum(_dot(qi, jnp.concatenate(tiles, axis=1)), 0.0) * w
            s = jnp.sum(d, axis=0, keepdims=True)
            for j in range(PAGES_PER_CHUNK):
                sc_ref[0, c, j:j + 1, :] = s[:, j * PAGE_SIZE:(j + 1) * PAGE_SIZE]
        return carry

    lax.fori_loop(0, n_pages // (PAGES_PER_CHUNK * IDX_CHUNK_UNROLL), chunk_body, 0)

    k_new = kinew_ref[0].astype(BF16).astype(F32)
    d_new = jnp.sum(qi.astype(F32) * k_new, axis=1, keepdims=True)
    s_new = jnp.sum(jnp.maximum(d_new, 0.0) * w, axis=0, keepdims=True)
    lane = lax.broadcasted_iota(jnp.int32, (1, PAGE_SIZE), 1)
    sc_ref[0, NEW_CHUNK] = jnp.full((SUBLANES, PAGE_SIZE), NEG_INF, F32)
    sc_ref[0, NEW_CHUNK, 0:1, :] = jnp.where(lane == 0, s_new, NEG_INF)


def _idx_sample(page_table, qi3, w3, ki_new3, cki, layer):
    n, n_pages = page_table.shape
    assert n_pages == (NEW_CHUNK * PAGES_PER_CHUNK)
    grid_spec = pltpu.PrefetchScalarGridSpec(
        num_scalar_prefetch=1,
        grid=(n,),
        in_specs=[
            pl.BlockSpec((1, IDX_HEADS, IDX_DIM), lambda b, pt: (b, 0, 0)),
            pl.BlockSpec((1, IDX_HEADS, 1), lambda b, pt: (b, 0, 0)),
            pl.BlockSpec((1, 1, IDX_DIM), lambda b, pt: (b, 0, 0)),
            pl.BlockSpec(memory_space=pl.ANY),
        ],
        out_specs=pl.BlockSpec((1, SC_CHUNKS, SUBLANES, PAGE_SIZE), lambda b, pt: (b, 0, 0, 0)),
        scratch_shapes=[pltpu.VMEM((2, n_pages, IDX_DIM, PAGE_SIZE), F32), pltpu.SemaphoreType.DMA((2,))],
    )
    return pl.pallas_call(
        functools.partial(_idx_sample_kernel, layer=layer),
        out_shape=jax.ShapeDtypeStruct((n, SC_CHUNKS, SUBLANES, PAGE_SIZE), F32),
        grid_spec=grid_spec,
        compiler_params=_cparams(("arbitrary",)),
        name="idx_sample",
    )(page_table, qi3, w3, ki_new3, cki)


def _topk_sample_kernel(sc_ref, thr_ref, cut_ref, *, topk):
    n = sc_ref.shape[0]
    shape4 = sc_ref.shape

    def total(x):
        x = jnp.sum(x, axis=1, keepdims=True)
        x = jnp.sum(x, axis=2, keepdims=True)
        return jnp.sum(x, axis=3, keepdims=True)

    kf = float(topk)

    def bisect_body(_, carry):
        lo, hi = carry
        mid = (lo >> 1) + (hi >> 1) + (lo & hi & 1)
        cnt = total(jnp.where(sc_ref[...] >= _key_to_float(mid), 1.0, 0.0))
        ge = cnt >= kf
        return jnp.where(ge, mid, lo), jnp.where(ge, hi, mid)

    lo0 = jnp.full((n, 1, 1, 1), KEY_NEG_INF + 1, jnp.int32)
    hi0 = jnp.full((n, 1, 1, 1), KEY_POS_INF + 1, jnp.int32)
    lo, _ = lax.fori_loop(0, BISECT_ITERS, bisect_body, (lo0, hi0))
    thr = _key_to_float(lo)

    kidx = (lax.broadcasted_iota(jnp.int32, shape4, 1) * (SUBLANES * PAGE_SIZE)
            + lax.broadcasted_iota(jnp.int32, shape4, 2) * PAGE_SIZE
            + lax.broadcasted_iota(jnp.int32, shape4, 3))

    def count_sel(cut):
        s = sc_ref[...]
        sel = (s > thr) | ((s == thr) & (kidx <= cut))
        return total(jnp.where(sel, 1.0, 0.0))

    n_keys = SC_CHUNKS * SUBLANES * PAGE_SIZE

    def cut_body(_, carry):
        lo_c, hi_c = carry
        mid = (lo_c + hi_c) >> 1
        ge = count_sel(mid) >= kf
        return jnp.where(ge, lo_c, mid), jnp.where(ge, mid, hi_c)

    lo_c = jnp.full((n, 1, 1, 1), -1, jnp.int32)
    hi_c = jnp.full((n, 1, 1, 1), n_keys - 1, jnp.int32)
    _, cut = lax.fori_loop(0, 15, cut_body, (lo_c, hi_c))
    thr_ref[...] = thr
    cut_ref[...] = cut


def _topk_sample(sc, topk):
    n = sc.shape[0]
    return pl.pallas_call(
        functools.partial(_topk_sample_kernel, topk=topk),
        out_shape=[jax.ShapeDtypeStruct((n, 1, 1, 1), F32), jax.ShapeDtypeStruct((n, 1, 1, 1), jnp.int32)],
        compiler_params=pltpu.CompilerParams(vmem_limit_bytes=VMEM_LIMIT),
        name="topk_sample",
    )(sc)


def _attn_sample_kernel(pt_ref, q_ref, knew_ref, vnew_ref, sc_ref, thr_ref, cut_ref, ck_ref, cv_ref, o_ref,
                        buf, lg_ref, qb_ref, sems, *, layer):
    b = pl.program_id(0)
    nb = pl.num_programs(0)
    n_pages = lg_ref.shape[1]
    n_ch = n_pages // ATT_PAGES

    def chunk_copy(src_ref, bb, u, j, slot):
        return pltpu.make_async_copy(src_ref.at[layer, pt_ref[bb, u * ATT_PAGES + j]], buf.at[slot, j], sems.at[slot])

    def start_chunk(src_ref, bb, u, slot):
        for j in range(ATT_PAGES):
            chunk_copy(src_ref, bb, u, j, slot).start()

    def wait_chunk(src_ref, bb, u, slot):
        for j in range(ATT_PAGES):
            chunk_copy(src_ref, bb, u, j, slot).wait()

    per_seq = 2 * n_ch
    n_flat = nb * per_seq

    def start_flat(g):
        bb = g // per_seq
        c = g % per_seq
        slot = g % ATT_SLOTS

        @pl.when(c < n_ch)
        def _():
            start_chunk(ck_ref, bb, c, slot)

        @pl.when(c >= n_ch)
        def _():
            start_chunk(cv_ref, bb, c - n_ch, slot)

    def prefetch(g):
        @pl.when(g + ATT_SLOTS - 1 < n_flat)
        def _():
            start_flat(g + ATT_SLOTS - 1)

    @pl.when(b == 0)
    def _():
        for g0 in range(ATT_SLOTS - 1):
            start_flat(jnp.int32(g0))

    for hh in range(ATT_HEADS):
        qb_ref[hh] = jnp.broadcast_to(q_ref[0, hh], (HEAD_DIM, PAGE_SIZE))

    def k_body(u, carry):
        g = b * per_seq + u
        slot = g % ATT_SLOTS
        wait_chunk(ck_ref, b, u, slot)
        prefetch(g)
        for j in range(ATT_PAGES):
            for hh in range(ATT_HEADS):
                lg = jnp.sum(buf[slot, j, hh] * qb_ref[hh], axis=0, keepdims=True)
                lg_ref[hh, pl.ds(u * ATT_PAGES + j, 1), :] = lg
        return carry

    lax.fori_loop(0, n_ch, k_body, 0)

    rows = SC_CHUNKS * SUBLANES
    s2 = sc_ref[0].reshape(rows, PAGE_SIZE)
    thr = thr_ref[0, 0]
    cut = cut_ref[0, 0]
    kidx = (lax.broadcasted_iota(jnp.int32, (rows, PAGE_SIZE), 0) * PAGE_SIZE
            + lax.broadcasted_iota(jnp.int32, (rows, PAGE_SIZE), 1))
    sel = jnp.where((s2 > thr) | ((s2 == thr) & (kidx <= cut)), 1.0, 0.0)
    sel_past = sel[0:n_pages, :] > 0.0
    sel_new = sel[n_pages:n_pages + 1, 0:1] > 0.0
    e_new, den = [], []
    for hh in range(ATT_HEADS):
        lgm = jnp.where(sel_past, lg_ref[hh], NEG_INF)
        l_new = jnp.sum(knew_ref[0, hh] * q_ref[0, hh], axis=0, keepdims=True)
        l_new = jnp.where(sel_new, l_new, NEG_INF)
        m = jnp.maximum(jnp.max(jnp.max(lgm, axis=1, keepdims=True), axis=0, keepdims=True), l_new)
        e = jnp.exp(lgm - m)
        lg_ref[hh] = e
        e_new.append(jnp.exp(l_new - m))
        den.append(jnp.sum(jnp.sum(e, axis=1, keepdims=True), axis=0, keepdims=True) + e_new[hh])

    def v_body(u, accs):
        g = b * per_seq + n_ch + u
        slot = g % ATT_SLOTS
        wait_chunk(cv_ref, b, u, slot)
        prefetch(g)
        accs = list(accs)
        for j in range(ATT_PAGES):
            for hh in range(ATT_HEADS):
                p_row = lg_ref[hh, pl.ds(u * ATT_PAGES + j, 1), :]
                accs[hh] = accs[hh] + buf[slot, j, hh] * p_row
        return tuple(accs)

    accs = lax.fori_loop(0, n_ch, v_body, tuple(jnp.zeros((HEAD_DIM, PAGE_SIZE), F32) for _ in range(ATT_HEADS)))
    for hh in range(ATT_HEADS):
        o = jnp.sum(accs[hh], axis=1, keepdims=True) + e_new[hh] * vnew_ref[0, hh]
        o_ref[0, hh] = o / den[hh]


def _attn_sample(page_table, q4, k_new4, v_new4, sc, thr, cut, ck_t, cv_t, layer):
    n, n_pages = page_table.shape
    col_blk = pl.BlockSpec((1, ATT_HEADS, HEAD_DIM, 1), lambda b, pt: (b, 0, 0, 0))
    one_blk = pl.BlockSpec((1, 1, 1, 1), lambda b, pt: (b, 0, 0, 0))
    grid_spec = pltpu.PrefetchScalarGridSpec(
        num_scalar_prefetch=1,
        grid=(n,),
        in_specs=[col_blk, col_blk, col_blk,
                  pl.BlockSpec((1, SC_CHUNKS, SUBLANES, PAGE_SIZE), lambda b, pt: (b, 0, 0, 0)),
                  one_blk, one_blk, pl.BlockSpec(memory_space=pl.ANY), pl.BlockSpec(memory_space=pl.ANY)],
        out_specs=col_blk,
        scratch_shapes=[pltpu.VMEM((ATT_SLOTS, ATT_PAGES, ATT_HEADS, HEAD_DIM, PAGE_SIZE), F32),
                        pltpu.VMEM((ATT_HEADS, n_pages, PAGE_SIZE), F32),
                        pltpu.VMEM((ATT_HEADS, HEAD_DIM, PAGE_SIZE), F32),
                        pltpu.SemaphoreType.DMA((ATT_SLOTS,))],
    )
    return pl.pallas_call(
        functools.partial(_attn_sample_kernel, layer=layer),
        out_shape=jax.ShapeDtypeStruct((n, ATT_HEADS, HEAD_DIM, 1), F32),
        grid_spec=grid_spec,
        compiler_params=_cparams(("arbitrary",)),
        name="attn_sample",
    )(page_table, q4, k_new4, v_new4, sc, thr, cut, ck_t, cv_t)


def _sample_layer(x, mod_s, lw, tabs_s, g_final, final, layer, ck_t, cv_t, cki_t, page_table, pool_prev, conv_prev):
    n = x.shape[0]
    n_pages = page_table.shape[1]
    (acd, q, qi, widx, k, v, ki, v_n, pool_new_t, conv_new_t) = _pre_sample(
        x, mod_s, lw, tabs_s, jnp.swapaxes(pool_prev, 0, 1), jnp.swapaxes(conv_prev, 0, 1))
    sc = _idx_sample(page_table, qi.reshape(n, IDX_HEADS, IDX_DIM), widx.reshape(n, IDX_HEADS, 1),
                     ki.reshape(n, 1, IDX_DIM), cki_t, layer)
    topk = min(TOPK_MAX, (n_pages * PAGE_SIZE + 1) // 4)
    thr, cut = _topk_sample(sc, topk)
    col = lambda a: a.reshape(n, ATT_HEADS, HEAD_DIM, 1)
    ob = _attn_sample(page_table, col(q), col(k), col(v), sc, thr, cut, ck_t, cv_t, layer)
    x_new = _post(x.reshape(1, n, D_MODEL), acd.reshape(1, n, 768), ob.reshape(1, n, 256).astype(BF16),
                  mod_s.reshape(1, 6, n, D_MODEL), lw, g_final, final).reshape(n, D_MODEL)
    return x_new, (k.reshape(n, 1, ATT_HEADS, HEAD_DIM), v.reshape(n, 1, ATT_HEADS, HEAD_DIM),
                   ki.reshape(n, 1, IDX_DIM), v_n.reshape(n, 1, W_GRP),
                   jnp.swapaxes(pool_new_t, 0, 1), jnp.swapaxes(conv_new_t, 0, 1))


def kernel(x_prompt, x_sample, cache_k, cache_v, cache_kidx, state_pool, state_conv, page_table, c_prompt, c_sample,
           w_mod, b_mod, g_attn_norm, g_mlp_norm, w_in, w_out, g_gmlp, w_spatial, b_spatial, w_pool, s_pool, w_dw, b_dw,
           g_conv_norm, b_conv_norm, w_up, w_down, g_final):
    B, S, _ = x_prompt.shape
    n = x_sample.shape[0]
    depth = w_mod.shape[0]
    past = page_table.shape[1] * PAGE_SIZE
    tabs_p = _tables(jnp.arange(S))
    tabs_s = _tables(jnp.full((1,), past, jnp.int32))
    g_fin = g_final.reshape(1, D_MODEL)
    pad = (-B) % SUBLANES
    c_rows = jnp.concatenate([c_prompt, jnp.zeros((pad, D_MODEL), F32), c_sample], axis=0)
    xp, xs = x_prompt, x_sample.reshape(n, D_MODEL)
    ck_t = jnp.transpose(cache_k, (0, 1, 3, 4, 2))
    cv_t = jnp.transpose(cache_v, (0, 1, 3, 4, 2))
    cki_t = jnp.transpose(cache_kidx, (0, 1, 3, 2))
    outs_p, outs_s = [], []
    for l in range(depth):
        lw = _layer_weights(l, w_mod, b_mod, g_attn_norm, g_mlp_norm, w_in, w_out, g_gmlp, w_spatial, b_spatial,
                            w_pool, s_pool, w_dw, b_dw, g_conv_norm, b_conv_norm, w_up, w_down)
        mod = _modulation(c_rows, lw["w_mod"], lw["b_mod"])
        mod_p = mod[:B].reshape(B, 6, D_MODEL)
        mod_s = jnp.swapaxes(mod[B + pad:].reshape(n, 6, D_MODEL), 0, 1)
        final = l == depth - 1
        xp, op = _prompt_layer(xp, mod_p, lw, tabs_p, g_fin, final)
        xs, os_ = _sample_layer(xs, mod_s, lw, tabs_s, g_fin, final, l, ck_t, cv_t, cki_t,
                                page_table, state_pool[l], state_conv[l])
        outs_p.append(op)
        outs_s.append(os_)
    stack = lambda outs, j: jnp.stack([o[j] for o in outs])
    return (xp, xs.reshape(n, 1, D_MODEL),
            stack(outs_p, 0), stack(outs_p, 1), stack(outs_p, 2), stack(outs_p, 3), stack(outs_p, 4),
            stack(outs_s, 0), stack(outs_s, 1), stack(outs_s, 2), stack(outs_s, 3), stack(outs_s, 4), stack(outs_s, 5))
```

```python
import functools

import jax
import jax.numpy as jnp
import numpy as np
from jax import lax
from jax.experimental import pallas as pl
from jax.experimental.pallas import tpu as pltpu

D_MODEL = 1024
W_GRP = 256
HEAD_DIM = 64
A_HEADS = 4
ATT_HEADS = 4
IDX_HEADS = 8
IDX_DIM = 64
CHUNK = 128
TOPK_MAX = 256
PAGE_SIZE = 128
ROPE_THETA = 10000.0
POOL_WINDOWS = (2, 4, 8, 16)
POOL_STATE = 15
CONV_WIDTH = 31
CONV_STATE = 30
CONV_GROUPS = 4
D_FF = 4 * D_MODEL
EPS = 1e-6
IN_SIZES = (256, 256, 256, 256, 256, 512, 64, 8, 256, 256, 256)
D_IN = sum(IN_SIZES)
KIW_OFF = 1792
CX_OFF_RAW = 1864
D_IN_PAD = 2688
CX_OFF, DA_OFF, DG_OFF = 1920, 2176, 2432
WT_Q, WT_QI, WT_V, WT_W, WT_ROWS = 0, 256, 768, 1024, 1040

LANES = 128
SUBLANES = 8
VMEM_LIMIT = 56 * 1024 * 1024

PRE_TM = 256
POST_TM = 512
FF_CHUNK = 1024
QB = 256
KT = 128
KT_UNROLL = 4
BISECT_ITERS = 32
PACK16 = 16
INT16_MIN = -32768
POOL_HALO = 16
CONV_HALO = 32

F32 = jnp.float32
BF16 = jnp.bfloat16
NEG_INF = float("-inf")
KEY_NEG_INF = -2139095041
KEY_POS_INF = 2139095040


def _cparams(sem):
    return pltpu.CompilerParams(dimension_semantics=sem, vmem_limit_bytes=VMEM_LIMIT)


def _dot(a, b):
    return jnp.dot(a, b, preferred_element_type=F32)


def _dot_nt(a, b):
    return lax.dot_general(a, b, (((1,), (1,)), ((), ())), preferred_element_type=F32)


def _group_mean(x, gmat):
    hi = x.astype(BF16)
    lo = (x - hi.astype(F32)).astype(BF16)
    return _dot(hi, gmat) + _dot(lo, gmat)


def _rot_half(x):
    n = x.shape[-1]
    lane = lax.broadcasted_iota(jnp.int32, x.shape, x.ndim - 1)
    fwd = pltpu.roll(x, n - HEAD_DIM // 2, x.ndim - 1)
    bwd = pltpu.roll(x, HEAD_DIM // 2, x.ndim - 1)
    return jnp.where((lane % HEAD_DIM) < HEAD_DIM // 2, fwd, bwd)


def _rope(x, cos, sin_signed):
    reps = x.shape[-1] // LANES
    if reps > 1:
        cos = jnp.concatenate([cos] * reps, axis=-1)
        sin_signed = jnp.concatenate([sin_signed] * reps, axis=-1)
    return x * cos + _rot_half(x) * sin_signed


def _sigmoid(x):
    return 1.0 / (1.0 + jnp.exp(-x))


def _mod_kernel(c_ref, w_ref, b_ref, o_ref):
    c = c_ref[...]
    a = (c * _sigmoid(c)).astype(BF16)
    o_ref[...] = _dot(a, w_ref[...].astype(BF16)) + b_ref[...]


def _modulation(c_rows, w_mod_l, b_mod_l):
    n = c_rows.shape[0]
    ncol = w_mod_l.shape[1]
    tn = 1024
    return pl.pallas_call(
        _mod_kernel,
        out_shape=jax.ShapeDtypeStruct((n, ncol), F32),
        grid=(ncol // tn,),
        in_specs=[
            pl.BlockSpec((n, D_MODEL), lambda j: (0, 0)),
            pl.BlockSpec((D_MODEL, tn), lambda j: (0, j)),
            pl.BlockSpec((1, tn), lambda j: (0, j)),
        ],
        out_specs=pl.BlockSpec((n, tn), lambda j: (0, j)),
        compiler_params=_cparams(("arbitrary",)),
        name="adaln_mod",
    )(c_rows, w_mod_l, b_mod_l.reshape(1, ncol))


def _modulated_norm(x, g, shift, scale):
    ms = jnp.mean(x * x, axis=-1, keepdims=True)
    y = x * lax.rsqrt(ms + EPS) * g
    return y * (1.0 + scale) + shift


def _head_rmsnorm(av, g_row, gmat):
    ms = _group_mean(av * av, gmat)
    return av * lax.rsqrt(ms + EPS) * g_row


def _group_layernorm_silu(y, g_row, b_row, gmat):
    mu = _group_mean(y, gmat)
    d = y - mu
    var = _group_mean(d * d, gmat)
    yn = d * lax.rsqrt(var + EPS) * g_row + b_row
    return yn * _sigmoid(yn)


def _rope_t(x, cos_t, sin_t):
    half = HEAD_DIM // 2
    out = []
    for hh in range(x.shape[0] // HEAD_DIM):
        a = x[hh * HEAD_DIM:hh * HEAD_DIM + half]
        b = x[hh * HEAD_DIM + half:(hh + 1) * HEAD_DIM]
        out.append(a * cos_t - b * sin_t)
        out.append(b * cos_t + a * sin_t)
    return jnp.concatenate(out, axis=0)


def _pre_prompt_kernel(x_ref, mod_ref, gattn_ref, win_ref, wint_ref, ggmlp_ref, wsp_ref, bsp_ref, cos_ref, sin_ref,
                       cost_ref, sint_ref, wpool_ref, spool_ref, wdw_ref, bdw_ref, gcn_ref, bcn_ref, gmat_ref,
                       acd_ref, qt_ref, qit_ref, wt_ref, kbf_ref, kibf_ref, vt_ref,
                       k_ref, v_ref, ki_ref, poolst_ref, convst_ref,
                       pool_buf, conv_buf):
    i = pl.program_id(1)
    tm = x_ref.shape[1]
    gmat = gmat_ref[...]

    @pl.when(i == 0)
    def _():
        pool_buf[0:POOL_HALO, :] = jnp.zeros((POOL_HALO, W_GRP), F32)
        conv_buf[0:CONV_HALO, :] = jnp.zeros((CONV_HALO, W_GRP), F32)

    mod = mod_ref[0]
    h = _modulated_norm(x_ref[0], gattn_ref[...], mod[0:1], mod[1:2]).astype(BF16)

    def proj(off, width):
        return _dot(h, win_ref[:, off:off + width])

    a_u = proj(0, 256)
    v_n = _head_rmsnorm(proj(256, 256), ggmlp_ref[...], gmat)
    v_bf = v_n.astype(BF16)
    lane256 = lax.broadcasted_iota(jnp.int32, (CHUNK, 256), 1)
    rr = lax.broadcasted_iota(jnp.int32, (CHUNK, CHUNK), 0)
    cc = lax.broadcasted_iota(jnp.int32, (CHUNK, CHUNK), 1)
    ws = [jnp.where(cc <= rr, wsp_ref[hh], 0.0).astype(BF16) for hh in range(A_HEADS)]
    gate_chunks = []
    for c in range(tm // CHUNK):
        vc = v_bf[c * CHUNK:(c + 1) * CHUNK, :]
        s = [_dot(ws[hh], vc) for hh in range(A_HEADS)]
        hid = lane256 // HEAD_DIM
        g = jnp.where(hid == 0, s[0], jnp.where(hid == 1, s[1], jnp.where(hid == 2, s[2], s[3])))
        gate_chunks.append(g + bsp_ref[...])
    out_a = a_u * jnp.concatenate(gate_chunks, axis=0)

    cos = cos_ref[...]
    sin = sin_ref[...]
    k = _rope(proj(768, 256), cos, sin)
    k_ref[0] = k
    kbf_ref[0] = k.astype(BF16)
    v_ref[0] = proj(1024, 256)
    ki = _rope(proj(KIW_OFF, LANES), cos, sin)[:, 0:IDX_DIM]
    ki_ref[0] = ki
    kibf_ref[0] = ki.astype(BF16)

    def proj_t(off, width):
        return _dot_nt(wint_ref[off:off + width, :], h)

    cos_t = cost_ref[...]
    sin_t = sint_ref[...]
    qt_ref[0] = (_rope_t(proj_t(WT_Q, 256), cos_t, sin_t) * (HEAD_DIM ** -0.5)).astype(BF16)
    qit_ref[0] = _rope_t(proj_t(WT_QI, 512), cos_t, sin_t).astype(BF16)
    v_t = proj_t(WT_V, 256).astype(BF16)
    for c in range(tm // KT):
        vt_ref[0, c] = v_t[:, c * KT:(c + 1) * KT]
    wt_ref[0] = proj_t(WT_W, 16)[0:IDX_HEADS, :] * ((IDX_HEADS * IDX_DIM) ** -0.5)

    c_x = proj(CX_OFF, 256)
    pool_buf[POOL_HALO:POOL_HALO + tm, :] = c_x
    lane128 = lax.broadcasted_iota(jnp.int32, (tm, LANES), 1)
    tpos = i * tm + lax.broadcasted_iota(jnp.int32, (tm, LANES), 0)
    upper = lane128 >= HEAD_DIM
    pooled = []
    for col, (w_lo, w_hi) in enumerate(((2, 4), (8, 16))):
        cs = slice(col * LANES, (col + 1) * LANES)
        s_lo = pool_buf[POOL_HALO:POOL_HALO + tm, cs]
        for j in range(1, w_lo):
            s_lo = s_lo + pool_buf[POOL_HALO - j:POOL_HALO - j + tm, cs]
        s_hi = pool_buf[POOL_HALO - w_lo:POOL_HALO - w_lo + tm, cs]
        for j in range(w_lo + 1, w_hi):
            s_hi = s_hi + pool_buf[POOL_HALO - j:POOL_HALO - j + tm, cs]
        win = jnp.where(upper, w_hi, w_lo)
        cnt = jnp.minimum(win, tpos + 1).astype(F32)
        pooled.append(jnp.where(upper, s_lo + s_hi, s_lo) / cnt)
    diff = (jnp.concatenate(pooled, axis=-1) - c_x).astype(BF16)
    out_c = _dot(diff, wpool_ref[...]) * spool_ref[...]
    poolst_ref[0] = pool_buf[tm:tm + POOL_HALO, :]
    pool_buf[0:POOL_HALO, :] = pool_buf[tm:tm + POOL_HALO, :]

    glu = proj(DA_OFF, 256) * _sigmoid(proj(DG_OFF, 256))
    conv_buf[CONV_HALO:CONV_HALO + tm, :] = glu
    base = CONV_HALO - CONV_STATE
    y = conv_buf[base:base + tm, :] * wdw_ref[0:1, :]
    for j in range(1, CONV_WIDTH):
        y = y + conv_buf[base + j:base + j + tm, :] * wdw_ref[j:j + 1, :]
    y = y + bdw_ref[...]
    out_d = _group_layernorm_silu(y, gcn_ref[...], bcn_ref[...], gmat)
    convst_ref[0] = conv_buf[tm:tm + CONV_HALO, :]
    conv_buf[0:CONV_HALO, :] = conv_buf[tm:tm + CONV_HALO, :]

    acd_ref[0] = jnp.concatenate([out_a, out_c, out_d], axis=-1).astype(BF16)


def _pre_prompt(x, mod, lw, tabs):
    B, S, _ = x.shape
    tm = min(PRE_TM, S)
    n_t = S // tm
    tok = lambda w: pl.BlockSpec((1, tm, w), lambda b, i: (b, i, 0))
    const = lambda shape: pl.BlockSpec(shape, lambda b, i: tuple(0 for _ in shape))
    in_specs = [
        tok(D_MODEL),
        pl.BlockSpec((1, 6, D_MODEL), lambda b, i: (b, 0, 0)),
        const((1, D_MODEL)),
        const((D_MODEL, D_IN_PAD)),
        const((WT_ROWS, D_MODEL)),
        const((1, W_GRP)),
        const((A_HEADS, CHUNK, CHUNK)),
        const((CHUNK, W_GRP)),
        pl.BlockSpec((tm, LANES), lambda b, i: (i, 0)),
        pl.BlockSpec((tm, LANES), lambda b, i: (i, 0)),
        pl.BlockSpec((HEAD_DIM // 2, tm), lambda b, i: (0, i)),
        pl.BlockSpec((HEAD_DIM // 2, tm), lambda b, i: (0, i)),
        const((W_GRP, W_GRP)),
        const((1, W_GRP)),
        const((32, W_GRP)),
        const((1, W_GRP)),
        const((1, W_GRP)),
        const((1, W_GRP)),
        const((W_GRP, W_GRP)),
    ]
    out_shapes = [
        jax.ShapeDtypeStruct((B, S, 768), BF16),
        jax.ShapeDtypeStruct((B, 256, S), BF16),
        jax.ShapeDtypeStruct((B, 512, S), BF16),
        jax.ShapeDtypeStruct((B, IDX_HEADS, S), F32),
        jax.ShapeDtypeStruct((B, S, 256), BF16),
        jax.ShapeDtypeStruct((B, S, IDX_DIM), BF16),
        jax.ShapeDtypeStruct((B, S // KT, 256, KT), BF16),
        jax.ShapeDtypeStruct((B, S, 256), F32),
        jax.ShapeDtypeStruct((B, S, 256), F32),
        jax.ShapeDtypeStruct((B, S, IDX_DIM), F32),
        jax.ShapeDtypeStruct((B, POOL_HALO, W_GRP), F32),
        jax.ShapeDtypeStruct((B, CONV_HALO, W_GRP), F32),
    ]
    feat = lambda w: pl.BlockSpec((1, w, tm), lambda b, i: (b, 0, i))
    out_specs = [tok(768), feat(256), feat(512), feat(IDX_HEADS), tok(256), tok(IDX_DIM),
                 pl.BlockSpec((1, tm // KT, 256, KT), lambda b, i: (b, i, 0, 0)),
                 tok(256), tok(256), tok(IDX_DIM),
                 pl.BlockSpec((1, POOL_HALO, W_GRP), lambda b, i: (b, 0, 0)),
                 pl.BlockSpec((1, CONV_HALO, W_GRP), lambda b, i: (b, 0, 0))]
    return pl.pallas_call(
        _pre_prompt_kernel,
        out_shape=out_shapes,
        grid=(B, n_t),
        in_specs=in_specs,
        out_specs=out_specs,
        scratch_shapes=[pltpu.VMEM((POOL_HALO + tm, W_GRP), F32), pltpu.VMEM((CONV_HALO + tm, W_GRP), F32)],
        compiler_params=_cparams(("arbitrary", "arbitrary")),
        name="pre_prompt",
    )(x, mod, lw["g_attn"], lw["w_in"], lw["w_in_t"], lw["g_gmlp"], lw["w_spatial"], lw["b_spatial_t"],
      tabs["cos"], tabs["sin"], tabs["cos_t"], tabs["sin_t"],
      lw["w_pool_bd"], lw["s_pool"], lw["w_dw"], lw["b_dw"], lw["g_cn"], lw["b_cn"], tabs["gmat"])


def _key_to_float(key):
    bits = jnp.where(key >= 0, key, key ^ jnp.int32(0x7FFFFFFF))
    return pltpu.bitcast(bits, F32)


def _float_to_key(x):
    bits = pltpu.bitcast(x, jnp.int32)
    return jnp.where(bits >= 0, bits, bits ^ jnp.int32(0x7FFFFFFF))


def _dsa_prompt_kernel(qt_ref, qit_ref, wt_ref, ki_ref, k_ref, vt_ref, o_ref, sc_ref, lg_ref, h16_ref, l16_ref,
                       *, topk):
    i = pl.program_id(1)
    qb = qt_ref.shape[2]
    blk_tiles = qb // KT
    n_real = (i + 1) * blk_tiles
    n_it = (n_real + KT_UNROLL - 1) // KT_UNROLL
    krow = lax.broadcasted_iota(jnp.int32, (KT, qb), 0)
    q_pos = i * qb + lax.broadcasted_iota(jnp.int32, (KT, qb), 1)
    q_pos1 = i * qb + lax.broadcasted_iota(jnp.int32, (1, qb), 1)

    def fold_sum(x):
        return jnp.sum(x.reshape(KT // SUBLANES, SUBLANES, qb), axis=0)

    def fold_max(x):
        return jnp.max(x.reshape(KT // SUBLANES, SUBLANES, qb), axis=0)

    def fold16(x):
        parts = [x[j * PACK16:(j + 1) * PACK16] for j in range(KT // PACK16)]
        while len(parts) > 1:
            parts = [a + b for a, b in zip(parts[0::2], parts[1::2])]
        return parts[0]

    def tile_of(it, u):
        return it * KT_UNROLL + u

    def key_rows(kt, n=KT):
        return pl.ds(pl.multiple_of(kt * KT, KT), n)

    w_rows = [wt_ref[0, hh:hh + 1, :] for hh in range(IDX_HEADS)]

    def score_tiles(kt0, diagonal):
        kc = ki_ref[0, key_rows(kt0, blk_tiles * KT), :]
        acc_all = jnp.zeros((blk_tiles * KT, qb), F32)
        for hh in range(IDX_HEADS):
            d = _dot(kc, qit_ref[0, hh * IDX_DIM:(hh + 1) * IDX_DIM, :])
            acc_all = acc_all + w_rows[hh] * jnp.maximum(d, 0.0)
        for u in range(blk_tiles):
            kt = kt0 + u
            acc = acc_all[u * KT:(u + 1) * KT]
            hi16 = _float_to_key(acc) >> 16
            if diagonal:
                causal = kt * KT + krow <= q_pos
                acc = jnp.where(causal, acc, NEG_INF)
                hi16 = jnp.where(causal, hi16, INT16_MIN)
            sc_ref[kt] = acc
            h16_ref[kt] = hi16.astype(jnp.int16)

    def score_body(it, carry):
        score_tiles(it * blk_tiles, False)
        return carry

    lax.fori_loop(0, i, score_body, 0)
    score_tiles(i * blk_tiles, True)

    @pl.when(n_it * KT_UNROLL > n_real)
    def _():
        for u in range(blk_tiles):
            sc_ref[n_real + u] = jnp.full((KT, qb), NEG_INF, F32)
            h16_ref[n_real + u] = jnp.full((KT, qb), INT16_MIN, jnp.int16)

    def count16(ref, t):
        t16 = t.astype(jnp.int16)

        def body(it, acc):
            for u in range(KT_UNROLL):
                hit = jnp.where(ref[tile_of(it, u)] >= t16, jnp.int16(1), jnp.int16(0))
                acc = acc + fold16(hit)
            return acc
        acc = lax.fori_loop(0, n_it, body, jnp.zeros((PACK16, qb), jnp.int16))
        return jnp.sum(acc.astype(jnp.int32), axis=0, keepdims=True)

    def count_where(pred):
        def body(it, acc):
            for u in range(KT_UNROLL):
                kt = tile_of(it, u)
                acc = acc + fold_sum(jnp.where(pred(sc_ref[kt], kt), 1.0, 0.0))
            return acc
        acc = lax.fori_loop(0, n_it, body, jnp.zeros((SUBLANES, qb), F32))
        return jnp.sum(acc, axis=0, keepdims=True).astype(jnp.int32)

    k_row = jnp.minimum(topk, q_pos1 + 1)

    def upper_body(_, carry):
        lo, hi, cnt_hi = carry
        mid = (lo + hi) >> 1
        cnt = count16(h16_ref, mid)
        ge = cnt >= k_row
        return jnp.where(ge, mid, lo), jnp.where(ge, hi, mid), jnp.where(ge, cnt_hi, cnt)

    full = lambda v: jnp.full((1, qb), v, jnp.int32)
    t_hi, _, n_above = lax.fori_loop(0, 16, upper_body, (full(INT16_MIN + 1), full(-INT16_MIN), full(0)))

    def lower_prep(it, carry):
        for u in range(KT_UNROLL):
            kt = tile_of(it, u)
            s = sc_ref[kt]
            key = _float_to_key(s)
            low = jnp.where((key >> 16) == t_hi, (key & 0xFFFF) >> 1, -1)
            l16_ref[kt] = jnp.where(s == NEG_INF, -1, low).astype(jnp.int16)
        return carry

    lax.fori_loop(0, n_it, lower_prep, 0)
    k_low = k_row - n_above

    def lower_body(_, carry):
        lo, hi = carry
        mid = (lo + hi) >> 1
        ge = count16(l16_ref, mid) >= k_low
        return jnp.where(ge, mid, lo), jnp.where(ge, hi, mid)

    t_lo, _ = lax.fori_loop(0, 15, lower_body, (full(0), full(-INT16_MIN)))
    key_even = (t_hi << 16) | (t_lo << 1)
    count_ge_key = lambda key: count_where(lambda s, kt: s >= _key_to_float(key))
    key_thr = jnp.where(count_ge_key(key_even + 1) >= k_row, key_even + 1, key_even)
    thr_packed = _key_to_float(key_thr)

    def ge_gt_body(it, carry):
        ge, gt = carry
        for u in range(KT_UNROLL):
            s = sc_ref[tile_of(it, u)]
            ge = ge + fold_sum(jnp.where(s >= thr_packed, 1.0, 0.0))
            gt = gt + fold_sum(jnp.where(s > thr_packed, 1.0, 0.0))
        return ge, gt

    zero8 = jnp.zeros((SUBLANES, qb), F32)
    ge8, gt8 = lax.fori_loop(0, n_it, ge_gt_body, (zero8, zero8))
    n_ge = jnp.sum(ge8, axis=0, keepdims=True).astype(jnp.int32)
    n_gt = jnp.sum(gt8, axis=0, keepdims=True).astype(jnp.int32)
    consistent = (n_ge >= k_row) & (n_gt < k_row)

    def compare_bisect(_):
        def body(_, carry):
            lo, hi = carry
            mid = (lo >> 1) + (hi >> 1) + (lo & hi & 1)
            ge = count_ge_key(mid) >= k_row
            return jnp.where(ge, mid, lo), jnp.where(ge, hi, mid)
        lo, _ = lax.fori_loop(0, BISECT_ITERS, body, (full(KEY_NEG_INF + 1), full(KEY_POS_INF + 1)))
        return lo, count_ge_key(lo)

    key_thr, n_ge = lax.cond(jnp.max(jnp.where(consistent, 0.0, 1.0)) > 0.0, compare_bisect,
                             lambda _: (key_thr, n_ge), 0)
    thr = _key_to_float(key_thr)

    def selected(s, kt, cut):
        return (s > thr) | ((s == thr) & (kt * KT + krow <= cut))

    n_keys = n_it * KT_UNROLL * KT
    any_excess = jnp.max(jnp.where(n_ge > k_row, 1.0, 0.0)) > 0.0

    def tie_cut(_):
        def body(_, carry):
            lo_c, hi_c = carry
            mid = (lo_c + hi_c) >> 1
            ge = count_where(lambda s, kt: selected(s, kt, mid)) >= k_row
            return jnp.where(ge, lo_c, mid), jnp.where(ge, mid, hi_c)
        lo_c = jnp.full((1, qb), -1, jnp.int32)
        hi_c = jnp.full((1, qb), 1, jnp.int32) * (n_keys - 1)
        _, hi_c = lax.fori_loop(0, 13, body, (lo_c, hi_c))
        return hi_c

    cut = lax.cond(any_excess, tie_cut, lambda _: jnp.full((1, qb), 1, jnp.int32) * n_keys, 0)

    def bias_body(it, carry):
        for u in range(KT_UNROLL):
            kt = tile_of(it, u)
            sc_ref[kt] = jnp.where(selected(sc_ref[kt], kt, cut), 0.0, NEG_INF)
        return carry

    lax.fori_loop(0, n_it, bias_body, 0)

    pair_row = lax.broadcasted_iota(jnp.int32, (2 * HEAD_DIM, qb), 0)
    n_pairs = ATT_HEADS // 2
    q2s = []
    for p in range(n_pairs):
        slab = qt_ref[0, p * LANES:(p + 1) * LANES, :].astype(F32)
        q2s.append(jnp.concatenate([jnp.where(pair_row < HEAD_DIM, slab, 0.0),
                                    jnp.where(pair_row >= HEAD_DIM, slab, 0.0)], axis=1).astype(BF16))

    def logit_step(p, it, mruns):
        mruns = list(mruns)
        lg4 = _dot(k_ref[0, key_rows(it * KT_UNROLL, KT_UNROLL * KT), p * LANES:(p + 1) * LANES], q2s[p])
        for u in range(KT_UNROLL):
            kt = tile_of(it, u)
            for half in range(2):
                lgt = lg4[u * KT:(u + 1) * KT, half * qb:(half + 1) * qb] + sc_ref[kt]
                lg_ref[2 * p + half, kt] = lgt
                mruns[half] = jnp.maximum(mruns[half], fold_max(lgt))
        return tuple(mruns)

    def pv_step(p, mx, it, carry):
        lruns, accs = list(carry[0]), list(carry[1])
        for half in range(2):
            hh = 2 * p + half
            pes = []
            for u in range(KT_UNROLL):
                pe = jnp.exp(lg_ref[hh, tile_of(it, u)] - mx[half])
                lruns[half] = lruns[half] + fold_sum(pe)
                pes.append(pe.astype(BF16))
            v4 = jnp.concatenate([vt_ref[0, tile_of(it, u), hh * HEAD_DIM:(hh + 1) * HEAD_DIM, :]
                                  for u in range(KT_UNROLL)], axis=1)
            accs[half] = accs[half] + _dot(v4, jnp.concatenate(pes, axis=0))
        return tuple(lruns), tuple(accs)

    neg = jnp.full((SUBLANES, qb), NEG_INF, F32)
    zl = jnp.zeros((SUBLANES, qb), F32)
    za = jnp.zeros((HEAD_DIM, qb), F32)
    row_max = lambda mruns: [jnp.max(m, axis=0, keepdims=True) for m in mruns]
    pv_init = ((zl, zl), (za, za))

    mx0 = row_max(lax.fori_loop(0, n_it, lambda it, m: logit_step(0, it, m), (neg, neg)))
    mruns1, pv0 = lax.fori_loop(
        0, n_it, lambda it, c: (logit_step(1, it, c[0]), pv_step(0, mx0, it, c[1])), ((neg, neg), pv_init))
    mx1 = row_max(mruns1)
    pv1 = lax.fori_loop(0, n_it, lambda it, c: pv_step(1, mx1, it, c), pv_init)

    o_heads = []
    for lruns, accs in (pv0, pv1):
        for half in range(2):
            o_heads.append(accs[half] / jnp.sum(lruns[half], axis=0, keepdims=True))
    o_ref[0] = jnp.concatenate(o_heads, axis=0).T.astype(BF16)


def _dsa_prompt(qt, qit, wt, kibf, kbf, vt):
    B, _, S = qt.shape
    qb = min(QB, S)
    n_kt = S // KT
    assert S % (KT * KT_UNROLL) == 0 and qb % KT == 0
    feat = lambda w: pl.BlockSpec((1, w, qb), lambda b, i: (b, 0, i))
    full = lambda w: pl.BlockSpec((1, S, w), lambda b, i: (b, 0, 0))
    return pl.pallas_call(
        functools.partial(_dsa_prompt_kernel, topk=min(TOPK_MAX, S // 4)),
        out_shape=jax.ShapeDtypeStruct((B, S, 256), BF16),
        grid=(B, S // qb),
        in_specs=[feat(256), feat(512), feat(IDX_HEADS), full(IDX_DIM), full(256),
                  pl.BlockSpec((1, n_kt, 256, KT), lambda b, i: (b, 0, 0, 0))],
        out_specs=pl.BlockSpec((1, qb, 256), lambda b, i: (b, i, 0)),
        scratch_shapes=[pltpu.VMEM((n_kt, KT, qb), F32), pltpu.VMEM((ATT_HEADS, n_kt, KT, qb), F32),
                        pltpu.VMEM((n_kt, KT, qb), jnp.int16), pltpu.VMEM((n_kt, KT, qb), jnp.int16)],
        compiler_params=_cparams(("arbitrary", "arbitrary")),
        name="dsa_prompt",
    )(qt, qit, wt, kibf, kbf, vt)


def _post_kernel(x_ref, acd_ref, ob_ref, mod_ref, gmlp_ref, woacd_ref, wob_ref, wup_ref, wdn_ref, gfin_ref, o_ref,
                 *, final):
    mod = mod_ref[0]
    mixed = _dot(acd_ref[0], woacd_ref[...]) + _dot(ob_ref[0], wob_ref[...])
    x1 = x_ref[0] + mod[2] * mixed
    h2 = _modulated_norm(x1, gmlp_ref[...], mod[3], mod[4]).astype(BF16)
    acc = jnp.zeros_like(x1)
    for c in range(D_FF // FF_CHUNK):
        u = jnp.maximum(_dot(h2, wup_ref[:, c * FF_CHUNK:(c + 1) * FF_CHUNK]), 0.0)
        acc = acc + _dot((u * u).astype(BF16), wdn_ref[c * FF_CHUNK:(c + 1) * FF_CHUNK, :])
    x2 = x1 + mod[5] * acc
    if final:
        ms = jnp.mean(x2 * x2, axis=-1, keepdims=True)
        x2 = x2 * lax.rsqrt(ms + EPS) * gfin_ref[...]
    o_ref[0] = x2


def _post(x, acd, ob, mod, lw, g_final, final):
    G, T, _ = x.shape
    tm = min(POST_TM, T)
    R = mod.shape[2]
    tok = lambda w: pl.BlockSpec((1, tm, w), lambda b, i: (b, i, 0))
    const = lambda shape: pl.BlockSpec(shape, lambda b, i: tuple(0 for _ in shape), pipeline_mode=pl.Buffered(1))
    if R == 1:
        mod_spec = pl.BlockSpec((1, 6, 1, D_MODEL), lambda b, i: (b, 0, 0, 0))
    else:
        mod_spec = pl.BlockSpec((1, 6, tm, D_MODEL), lambda b, i: (b, 0, i, 0))
    return pl.pallas_call(
        functools.partial(_post_kernel, final=final),
        out_shape=jax.ShapeDtypeStruct((G, T, D_MODEL), F32),
        grid=(G, T // tm),
        in_specs=[tok(D_MODEL), tok(768), tok(256), mod_spec, const((1, D_MODEL)),
                  const((768, D_MODEL)), const((256, D_MODEL)), const((D_MODEL, D_FF)), const((D_FF, D_MODEL)),
                  const((1, D_MODEL))],
        out_specs=tok(D_MODEL),
        compiler_params=_cparams(("arbitrary", "arbitrary")),
        name="post_final" if final else "post",
    )(x, acd, ob, mod, lw["g_mlp"], lw["w_out_acd"], lw["w_out_b"], lw["w_up"], lw["w_down"], g_final)


def _layer_weights(l, w_mod, b_mod, g_attn_norm, g_mlp_norm, w_in, w_out, g_gmlp, w_spatial, b_spatial,
                   w_pool, s_pool, w_dw, b_dw, g_conv_norm, b_conv_norm, w_up, w_down):
    w_in_l = w_in[l]
    w_in_pad = jnp.concatenate(
        [w_in_l[:, :CX_OFF_RAW], jnp.zeros((D_MODEL, CX_OFF - CX_OFF_RAW), w_in_l.dtype), w_in_l[:, CX_OFF_RAW:]],
        axis=1).astype(BF16)
    w_pool_bd = jnp.zeros((W_GRP, W_GRP), F32)
    for g in range(len(POOL_WINDOWS)):
        w_pool_bd = w_pool_bd.at[g * 64:(g + 1) * 64, g * 64:(g + 1) * 64].set(w_pool[l, g])
    w_out_l = w_out[l]
    w_in_t = jnp.concatenate(
        [w_in_l[:, 512:768], w_in_l[:, 1280:1792], w_in_l[:, 1024:1280], w_in_l[:, 1856:1864],
         jnp.zeros((D_MODEL, WT_ROWS - WT_W - IDX_HEADS), w_in_l.dtype)], axis=1).T.astype(BF16)
    return dict(
        w_in_t=w_in_t,
        w_mod=w_mod[l], b_mod=b_mod[l],
        g_attn=g_attn_norm[l].reshape(1, D_MODEL), g_mlp=g_mlp_norm[l].reshape(1, D_MODEL),
        w_in=w_in_pad,
        g_gmlp=g_gmlp[l].reshape(1, W_GRP),
        w_spatial=w_spatial[l],
        b_spatial_t=jnp.repeat(b_spatial[l].T, HEAD_DIM, axis=1),
        w_sp0=jnp.repeat(w_spatial[l, :, 0, 0], HEAD_DIM).reshape(1, W_GRP),
        b_sp0=jnp.repeat(b_spatial[l, :, 0], HEAD_DIM).reshape(1, W_GRP),
        w_pool_bd=w_pool_bd.astype(BF16), s_pool=s_pool[l].reshape(1, W_GRP),
        w_dw=jnp.concatenate([w_dw[l], jnp.zeros((1, W_GRP), F32)], axis=0), b_dw=b_dw[l].reshape(1, W_GRP),
        g_cn=g_conv_norm[l].reshape(1, W_GRP), b_cn=b_conv_norm[l].reshape(1, W_GRP),
        w_out_acd=jnp.concatenate([w_out_l[0:256], w_out_l[512:1024]], axis=0).astype(BF16),
        w_out_b=w_out_l[256:512].astype(BF16),
        w_up=w_up[l].astype(BF16), w_down=w_down[l].astype(BF16),
    )


def _tables(pos):
    half = HEAD_DIM // 2
    freqs = ROPE_THETA ** (-jnp.arange(half, dtype=F32) / half)
    ang = pos.astype(F32)[:, None] * freqs[None, :]
    cos = jnp.tile(jnp.cos(ang), (1, LANES // half))
    sin = jnp.sin(ang)
    sin_signed = jnp.tile(jnp.concatenate([-sin, sin], axis=1), (1, LANES // HEAD_DIM))
    gid = jnp.arange(W_GRP) // HEAD_DIM
    gmat = jnp.where(gid[:, None] == gid[None, :], 1.0 / HEAD_DIM, 0.0).astype(BF16)
    return dict(cos=cos, sin=sin_signed, gmat=gmat, cos_t=jnp.cos(ang).T, sin_t=sin.T)


def _prompt_layer(x, mod_p, lw, tabs, g_final, final):
    B, S, _ = x.shape
    (acd, qt, qit, wt, kbf, kibf, vt, k, v, ki, poolst, convst) = _pre_prompt(x, mod_p, lw, tabs)
    ob = _dsa_prompt(qt, qit, wt, kibf, kbf, vt)
    x_new = _post(x, acd, ob, mod_p.reshape(B, 6, 1, D_MODEL), lw, g_final, final)
    return x_new, (k.reshape(B, S, ATT_HEADS, HEAD_DIM), v.reshape(B, S, ATT_HEADS, HEAD_DIM), ki,
                   poolst[:, POOL_HALO - POOL_STATE:], convst[:, CONV_HALO - CONV_STATE:])


def _pre_sample_kernel(x_ref, mod_ref, gattn_ref, win_ref, ggmlp_ref, wsp0_ref, bsp0_ref, cos_ref, sin_ref,
                       wpool_ref, spool_ref, wdw_ref, bdw_ref, gcn_ref, bcn_ref, gmat_ref, poolprev_ref, convprev_ref,
                       acd_ref, q_ref, qi_ref, widx_ref, k_ref, v_ref, ki_ref, vn_ref, poolnew_ref, convnew_ref):
    gmat = gmat_ref[...]
    h = _modulated_norm(x_ref[...], gattn_ref[...], mod_ref[0], mod_ref[1]).astype(BF16)

    def proj(off, width):
        return _dot(h, win_ref[:, off:off + width])

    a_u = proj(0, 256)
    v_n = _head_rmsnorm(proj(256, 256), ggmlp_ref[...], gmat)
    vn_ref[...] = v_n
    out_a = a_u * (wsp0_ref[...] * v_n + bsp0_ref[...])

    cos = cos_ref[...]
    sin = sin_ref[...]
    q_ref[...] = _rope(proj(512, 256), cos, sin) * (HEAD_DIM ** -0.5)
    k_ref[...] = _rope(proj(768, 256), cos, sin)
    v_ref[...] = proj(1024, 256)
    qi_ref[...] = _rope(proj(1280, 512), cos, sin).astype(BF16)
    kiw = proj(KIW_OFF, LANES)
    ki_ref[...] = _rope(kiw, cos, sin)[:, 0:IDX_DIM]
    widx_ref[...] = kiw[:, IDX_DIM:IDX_DIM + IDX_HEADS] * ((IDX_HEADS * IDX_DIM) ** -0.5)

    c_x = proj(CX_OFF, 256)
    sums = {}
    run = c_x
    for j in range(1, max(POOL_WINDOWS)):
        run = run + poolprev_ref[POOL_STATE - j]
        if j + 1 in POOL_WINDOWS:
            sums[j + 1] = run * (1.0 / (j + 1))
    gid = lax.broadcasted_iota(jnp.int32, c_x.shape, 1) // HEAD_DIM
    pooled = jnp.where(gid == 0, sums[2], jnp.where(gid == 1, sums[4], jnp.where(gid == 2, sums[8], sums[16])))
    diff = (pooled - c_x).astype(BF16)
    out_c = _dot(diff, wpool_ref[...]) * spool_ref[...]
    for r in range(POOL_STATE - 1):
        poolnew_ref[r] = poolprev_ref[r + 1]
    poolnew_ref[POOL_STATE - 1] = c_x

    glu = proj(DA_OFF, 256) * _sigmoid(proj(DG_OFF, 256))
    y = glu * wdw_ref[CONV_STATE:CONV_STATE + 1, :]
    for j in range(CONV_STATE):
        y = y + convprev_ref[j] * wdw_ref[j:j + 1, :]
    y = y + bdw_ref[...]
    out_d = _group_layernorm_silu(y, gcn_ref[...], bcn_ref[...], gmat)
    for r in range(CONV_STATE - 1):
        convnew_ref[r] = convprev_ref[r + 1]
    convnew_ref[CONV_STATE - 1] = glu

    acd_ref[...] = jnp.concatenate([out_a, out_c, out_d], axis=-1).astype(BF16)


def _pre_sample(x, mod_s, lw, tabs_s, pool_prev_t, conv_prev_t):
    n = x.shape[0]
    out_shapes = [
        jax.ShapeDtypeStruct((n, 768), BF16),
        jax.ShapeDtypeStruct((n, 256), F32),
        jax.ShapeDtypeStruct((n, 512), BF16),
        jax.ShapeDtypeStruct((n, IDX_HEADS), F32),
        jax.ShapeDtypeStruct((n, 256), F32),
        jax.ShapeDtypeStruct((n, 256), F32),
        jax.ShapeDtypeStruct((n, IDX_DIM), F32),
        jax.ShapeDtypeStruct((n, 256), F32),
        jax.ShapeDtypeStruct((POOL_STATE, n, W_GRP), F32),
        jax.ShapeDtypeStruct((CONV_STATE, n, W_GRP), F32),
    ]
    return pl.pallas_call(
        _pre_sample_kernel,
        out_shape=out_shapes,
        compiler_params=pltpu.CompilerParams(vmem_limit_bytes=VMEM_LIMIT),
        name="pre_sample",
    )(x, mod_s, lw["g_attn"], lw["w_in"], lw["g_gmlp"], lw["w_sp0"], lw["b_sp0"], tabs_s["cos"], tabs_s["sin"],
      lw["w_pool_bd"], lw["s_pool"], lw["w_dw"], lw["b_dw"], lw["g_cn"], lw["b_cn"], tabs_s["gmat"],
      pool_prev_t, conv_prev_t)


SC_CHUNKS = 17
PAGES_PER_CHUNK = 8
NEW_CHUNK = 16
IDX_CHUNK_UNROLL = 4
ATT_PAGES = 16
ATT_SLOTS = 3


def _idx_sample_kernel(pt_ref, qi_ref, w_ref, kinew_ref, cki_ref, sc_ref, kbuf, sems, *, layer):
    b = pl.program_id(0)
    nb = pl.num_programs(0)
    n_pages = kbuf.shape[1]

    def page_copy(bb, p, slot):
        return pltpu.make_async_copy(cki_ref.at[layer, pt_ref[bb, p]], kbuf.at[slot, p], sems.at[slot])

    def start_all(bb, slot):
        def body(p, c):
            page_copy(bb, p, slot).start()
            return c
        lax.fori_loop(0, n_pages, body, 0)

    slot = b % 2

    @pl.when(b == 0)
    def _():
        start_all(0, 0)

    @pl.when(b + 1 < nb)
    def _():
        start_all(b + 1, 1 - slot)

    def wait_body(p, c):
        page_copy(b, p, slot).wait()
        return c
    lax.fori_loop(0, n_pages, wait_body, 0)

    qi = qi_ref[0]
    w = w_ref[0]

    def chunk_body(it, carry):
        for cc in range(IDX_CHUNK_UNROLL):
            c = it * IDX_CHUNK_UNROLL + cc
            tiles = [kbuf[slot, c * PAGES_PER_CHUNK + j].astype(BF16) for j in range(PAGES_PER_CHUNK)]
            d = jnp.maximum(_dot(qi, jnp.concatenate(tiles, axis=1)), 0.0) * w
            s = jnp.sum(d, axis=0, keepdims=True)
            for j in range(PAGES_PER_CHUNK):
                sc_ref[0, c, j:j + 1, :] = s[:, j * PAGE_SIZE:(j + 1) * PAGE_SIZE]
        return carry

    lax.fori_loop(0, n_pages // (PAGES_PER_CHUNK * IDX_CHUNK_UNROLL), chunk_body, 0)

    k_new = kinew_ref[0].astype(BF16).astype(F32)
    d_new = jnp.sum(qi.astype(F32) * k_new, axis=1, keepdims=True)
    s_new = jnp.sum(jnp.maximum(d_new, 0.0) * w, axis=0, keepdims=True)
    lane = lax.broadcasted_iota(jnp.int32, (1, PAGE_SIZE), 1)
    sc_ref[0, NEW_CHUNK] = jnp.full((SUBLANES, PAGE_SIZE), NEG_INF, F32)
    sc_ref[0, NEW_CHUNK, 0:1, :] = jnp.where(lane == 0, s_new, NEG_INF)


def _idx_sample(page_table, qi3, w3, ki_new3, cki, layer):
    n, n_pages = page_table.shape
    assert n_pages == (NEW_CHUNK * PAGES_PER_CHUNK)
    grid_spec = pltpu.PrefetchScalarGridSpec(
        num_scalar_prefetch=1,
        grid=(n,),
        in_specs=[
            pl.BlockSpec((1, IDX_HEADS, IDX_DIM), lambda b, pt: (b, 0, 0)),
            pl.BlockSpec((1, IDX_HEADS, 1), lambda b, pt: (b, 0, 0)),
            pl.BlockSpec((1, 1, IDX_DIM), lambda b, pt: (b, 0, 0)),
            pl.BlockSpec(memory_space=pl.ANY),
        ],
        out_specs=pl.BlockSpec((1, SC_CHUNKS, SUBLANES, PAGE_SIZE), lambda b, pt: (b, 0, 0, 0)),
        scratch_shapes=[pltpu.VMEM((2, n_pages, IDX_DIM, PAGE_SIZE), F32), pltpu.SemaphoreType.DMA((2,))],
    )
    return pl.pallas_call(
        functools.partial(_idx_sample_kernel, layer=layer),
        out_shape=jax.ShapeDtypeStruct((n, SC_CHUNKS, SUBLANES, PAGE_SIZE), F32),
        grid_spec=grid_spec,
        compiler_params=_cparams(("arbitrary",)),
        name="idx_sample",
    )(page_table, qi3, w3, ki_new3, cki)


def _topk_sample_kernel(sc_ref, thr_ref, cut_ref, *, topk):
    n = sc_ref.shape[0]
    shape4 = sc_ref.shape

    def total(x):
        x = jnp.sum(x, axis=1, keepdims=True)
        x = jnp.sum(x, axis=2, keepdims=True)
        return jnp.sum(x, axis=3, keepdims=True)

    kf = float(topk)

    def bisect_body(_, carry):
        lo, hi = carry
        mid = (lo >> 1) + (hi >> 1) + (lo & hi & 1)
        cnt = total(jnp.where(sc_ref[...] >= _key_to_float(mid), 1.0, 0.0))
        ge = cnt >= kf
        return jnp.where(ge, mid, lo), jnp.where(ge, hi, mid)

    lo0 = jnp.full((n, 1, 1, 1), KEY_NEG_INF + 1, jnp.int32)
    hi0 = jnp.full((n, 1, 1, 1), KEY_POS_INF + 1, jnp.int32)
    lo, _ = lax.fori_loop(0, BISECT_ITERS, bisect_body, (lo0, hi0))
    thr = _key_to_float(lo)

    kidx = (lax.broadcasted_iota(jnp.int32, shape4, 1) * (SUBLANES * PAGE_SIZE)
            + lax.broadcasted_iota(jnp.int32, shape4, 2) * PAGE_SIZE
            + lax.broadcasted_iota(jnp.int32, shape4, 3))

    def count_sel(cut):
        s = sc_ref[...]
        sel = (s > thr) | ((s == thr) & (kidx <= cut))
        return total(jnp.where(sel, 1.0, 0.0))

    n_keys = SC_CHUNKS * SUBLANES * PAGE_SIZE

    def cut_body(_, carry):
        lo_c, hi_c = carry
        mid = (lo_c + hi_c) >> 1
        ge = count_sel(mid) >= kf
        return jnp.where(ge, lo_c, mid), jnp.where(ge, mid, hi_c)

    lo_c = jnp.full((n, 1, 1, 1), -1, jnp.int32)
    hi_c = jnp.full((n, 1, 1, 1), n_keys - 1, jnp.int32)
    _, cut = lax.fori_loop(0, 15, cut_body, (lo_c, hi_c))
    thr_ref[...] = thr
    cut_ref[...] = cut


def _topk_sample(sc, topk):
    n = sc.shape[0]
    return pl.pallas_call(
        functools.partial(_topk_sample_kernel, topk=topk),
        out_shape=[jax.ShapeDtypeStruct((n, 1, 1, 1), F32), jax.ShapeDtypeStruct((n, 1, 1, 1), jnp.int32)],
        compiler_params=pltpu.CompilerParams(vmem_limit_bytes=VMEM_LIMIT),
        name="topk_sample",
    )(sc)


def _attn_sample_kernel(pt_ref, q_ref, knew_ref, vnew_ref, sc_ref, thr_ref, cut_ref, ck_ref, cv_ref, o_ref,
                        buf, lg_ref, qb_ref, sems, *, layer):
    b = pl.program_id(0)
    nb = pl.num_programs(0)
    n_pages = lg_ref.shape[1]
    n_ch = n_pages // ATT_PAGES

    def chunk_copy(src_ref, bb, u, j, slot):
        return pltpu.make_async_copy(src_ref.at[layer, pt_ref[bb, u * ATT_PAGES + j]], buf.at[slot, j], sems.at[slot])

    def start_chunk(src_ref, bb, u, slot):
        for j in range(ATT_PAGES):
            chunk_copy(src_ref, bb, u, j, slot).start(priority=j % 2)

    def wait_chunk(src_ref, bb, u, slot):
        for j in range(ATT_PAGES):
            chunk_copy(src_ref, bb, u, j, slot).wait()

    per_seq = 2 * n_ch
    n_flat = nb * per_seq

    def start_flat(g):
        bb = g // per_seq
        c = g % per_seq
        slot = g % ATT_SLOTS

        @pl.when(c < n_ch)
        def _():
            start_chunk(ck_ref, bb, c, slot)

        @pl.when(c >= n_ch)
        def _():
            start_chunk(cv_ref, bb, c - n_ch, slot)

    def prefetch(g):
        @pl.when(g + ATT_SLOTS - 1 < n_flat)
        def _():
            start_flat(g + ATT_SLOTS - 1)

    @pl.when(b == 0)
    def _():
        for g0 in range(ATT_SLOTS - 1):
            start_flat(jnp.int32(g0))

    for hh in range(ATT_HEADS):
        qb_ref[hh] = jnp.broadcast_to(q_ref[0, hh], (HEAD_DIM, PAGE_SIZE))

    def k_body(u, carry):
        g = b * per_seq + u
        slot = g % ATT_SLOTS
        wait_chunk(ck_ref, b, u, slot)
        prefetch(g)
        for j in range(ATT_PAGES):
            for hh in range(ATT_HEADS):
                lg = jnp.sum(buf[slot, j, hh] * qb_ref[hh], axis=0, keepdims=True)
                lg_ref[hh, pl.ds(u * ATT_PAGES + j, 1), :] = lg
        return carry

    lax.fori_loop(0, n_ch, k_body, 0)

    rows = SC_CHUNKS * SUBLANES
    s2 = sc_ref[0].reshape(rows, PAGE_SIZE)
    thr = thr_ref[0, 0]
    cut = cut_ref[0, 0]
    kidx = (lax.broadcasted_iota(jnp.int32, (rows, PAGE_SIZE), 0) * PAGE_SIZE
            + lax.broadcasted_iota(jnp.int32, (rows, PAGE_SIZE), 1))
    sel = jnp.where((s2 > thr) | ((s2 == thr) & (kidx <= cut)), 1.0, 0.0)
    sel_past = sel[0:n_pages, :] > 0.0
    sel_new = sel[n_pages:n_pages + 1, 0:1] > 0.0
    e_new, den = [], []
    for hh in range(ATT_HEADS):
        lgm = jnp.where(sel_past, lg_ref[hh], NEG_INF)
        l_new = jnp.sum(knew_ref[0, hh] * q_ref[0, hh], axis=0, keepdims=True)
        l_new = jnp.where(sel_new, l_new, NEG_INF)
        m = jnp.maximum(jnp.max(jnp.max(lgm, axis=1, keepdims=True), axis=0, keepdims=True), l_new)
        e = jnp.exp(lgm - m)
        lg_ref[hh] = e
        e_new.append(jnp.exp(l_new - m))
        den.append(jnp.sum(jnp.sum(e, axis=1, keepdims=True), axis=0, keepdims=True) + e_new[hh])

    def v_body(u, accs):
        g = b * per_seq + n_ch + u
        slot = g % ATT_SLOTS
        wait_chunk(cv_ref, b, u, slot)
        prefetch(g)
        accs = list(accs)
        for j in range(ATT_PAGES):
            for hh in range(ATT_HEADS):
                p_row = lg_ref[hh, pl.ds(u * ATT_PAGES + j, 1), :]
                accs[hh] = accs[hh] + buf[slot, j, hh] * p_row
        return tuple(accs)

    accs = lax.fori_loop(0, n_ch, v_body, tuple(jnp.zeros((HEAD_DIM, PAGE_SIZE), F32) for _ in range(ATT_HEADS)))
    for hh in range(ATT_HEADS):
        o = jnp.sum(accs[hh], axis=1, keepdims=True) + e_new[hh] * vnew_ref[0, hh]
        o_ref[0, hh] = o / den[hh]


def _attn_sample(page_table, q4, k_new4, v_new4, sc, thr, cut, ck_t, cv_t, layer):
    n, n_pages = page_table.shape
    col_blk = pl.BlockSpec((1, ATT_HEADS, HEAD_DIM, 1), lambda b, pt: (b, 0, 0, 0))
    one_blk = pl.BlockSpec((1, 1, 1, 1), lambda b, pt: (b, 0, 0, 0))
    grid_spec = pltpu.PrefetchScalarGridSpec(
        num_scalar_prefetch=1,
        grid=(n,),
        in_specs=[col_blk, col_blk, col_blk,
                  pl.BlockSpec((1, SC_CHUNKS, SUBLANES, PAGE_SIZE), lambda b, pt: (b, 0, 0, 0)),
                  one_blk, one_blk, pl.BlockSpec(memory_space=pl.ANY), pl.BlockSpec(memory_space=pl.ANY)],
        out_specs=col_blk,
        scratch_shapes=[pltpu.VMEM((ATT_SLOTS, ATT_PAGES, ATT_HEADS, HEAD_DIM, PAGE_SIZE), F32),
                        pltpu.VMEM((ATT_HEADS, n_pages, PAGE_SIZE), F32),
                        pltpu.VMEM((ATT_HEADS, HEAD_DIM, PAGE_SIZE), F32),
                        pltpu.SemaphoreType.DMA((ATT_SLOTS,))],
    )
    return pl.pallas_call(
        functools.partial(_attn_sample_kernel, layer=layer),
        out_shape=jax.ShapeDtypeStruct((n, ATT_HEADS, HEAD_DIM, 1), F32),
        grid_spec=grid_spec,
        compiler_params=_cparams(("arbitrary",)),
        name="attn_sample",
    )(page_table, q4, k_new4, v_new4, sc, thr, cut, ck_t, cv_t)


def _sample_layer(x, mod_s, lw, tabs_s, g_final, final, layer, ck_t, cv_t, cki_t, page_table, pool_prev, conv_prev):
    n = x.shape[0]
    n_pages = page_table.shape[1]
    (acd, q, qi, widx, k, v, ki, v_n, pool_new_t, conv_new_t) = _pre_sample(
        x, mod_s, lw, tabs_s, jnp.swapaxes(pool_prev, 0, 1), jnp.swapaxes(conv_prev, 0, 1))
    sc = _idx_sample(page_table, qi.reshape(n, IDX_HEADS, IDX_DIM), widx.reshape(n, IDX_HEADS, 1),
                     ki.reshape(n, 1, IDX_DIM), cki_t, layer)
    topk = min(TOPK_MAX, (n_pages * PAGE_SIZE + 1) // 4)
    thr, cut = _topk_sample(sc, topk)
    col = lambda a: a.reshape(n, ATT_HEADS, HEAD_DIM, 1)
    ob = _attn_sample(page_table, col(q), col(k), col(v), sc, thr, cut, ck_t, cv_t, layer)
    x_new = _post(x.reshape(1, n, D_MODEL), acd.reshape(1, n, 768), ob.reshape(1, n, 256).astype(BF16),
                  mod_s.reshape(1, 6, n, D_MODEL), lw, g_final, final).reshape(n, D_MODEL)
    return x_new, (k.reshape(n, 1, ATT_HEADS, HEAD_DIM), v.reshape(n, 1, ATT_HEADS, HEAD_DIM),
                   ki.reshape(n, 1, IDX_DIM), v_n.reshape(n, 1, W_GRP),
                   jnp.swapaxes(pool_new_t, 0, 1), jnp.swapaxes(conv_new_t, 0, 1))


def kernel(x_prompt, x_sample, cache_k, cache_v, cache_kidx, state_pool, state_conv, page_table, c_prompt, c_sample,
           w_mod, b_mod, g_attn_norm, g_mlp_norm, w_in, w_out, g_gmlp, w_spatial, b_spatial, w_pool, s_pool, w_dw, b_dw,
           g_conv_norm, b_conv_norm, w_up, w_down, g_final):
    B, S, _ = x_prompt.shape
    n = x_sample.shape[0]
    depth = w_mod.shape[0]
    past = page_table.shape[1] * PAGE_SIZE
    tabs_p = _tables(jnp.arange(S))
    tabs_s = _tables(jnp.full((1,), past, jnp.int32))
    g_fin = g_final.reshape(1, D_MODEL)
    pad = (-B) % SUBLANES
    c_rows = jnp.concatenate([c_prompt, jnp.zeros((pad, D_MODEL), F32), c_sample], axis=0)
    xp, xs = x_prompt, x_sample.reshape(n, D_MODEL)
    ck_t = jnp.transpose(cache_k, (0, 1, 3, 4, 2))
    cv_t = jnp.transpose(cache_v, (0, 1, 3, 4, 2))
    cki_t = jnp.transpose(cache_kidx, (0, 1, 3, 2))
    outs_p, outs_s = [], []
    for l in range(depth):
        lw = _layer_weights(l, w_mod, b_mod, g_attn_norm, g_mlp_norm, w_in, w_out, g_gmlp, w_spatial, b_spatial,
                            w_pool, s_pool, w_dw, b_dw, g_conv_norm, b_conv_norm, w_up, w_down)
        mod = _modulation(c_rows, lw["w_mod"], lw["b_mod"])
        mod_p = mod[:B].reshape(B, 6, D_MODEL)
        mod_s = jnp.swapaxes(mod[B + pad:].reshape(n, 6, D_MODEL), 0, 1)
        final = l == depth - 1
        xp, op = _prompt_layer(xp, mod_p, lw, tabs_p, g_fin, final)
        xs, os_ = _sample_layer(xs, mod_s, lw, tabs_s, g_fin, final, l, ck_t, cv_t, cki_t,
                                page_table, state_pool[l], state_conv[l])
        outs_p.append(op)
        outs_s.append(os_)
    stack = lambda outs, j: jnp.stack([o[j] for o in outs])
    return (xp, xs.reshape(n, 1, D_MODEL),
            stack(outs_p, 0), stack(outs_p, 1), stack(outs_p, 2), stack(outs_p, 3), stack(outs_p, 4),
            stack(outs_s, 0), stack(outs_s, 1), stack(outs_s, 2), stack(outs_s, 3), stack(outs_s, 4), stack(outs_s, 5))
```
